```python
import math
import jax, jax.numpy as jnp
from jax import lax
import numpy as np

D_MODEL = 1024
BATCH = 8
SEQ = 2048
DEPTH = 2
DEC_BATCH = 32
DEC_SEQ = 8
PAST_LEN = 8192
PAGE_SIZE = 128

F32 = jnp.float32
EPS = 1e-6
MIX_WIDTH = D_MODEL // 2
N_BRANCH = 3
SB_HEAD_DIM = 64
SB_HEADS = MIX_WIDTH // SB_HEAD_DIM
SB_WIDTH = SB_HEADS * SB_HEAD_DIM
Q_BLOCK = 128
SSM_HEAD_DIM = 64
SSM_WIDTH = MIX_WIDTH
SSM_HEADS = SSM_WIDTH // SSM_HEAD_DIM
SSM_GROUPS = 2
SSM_STATE = 128
SSM_CONV = 4
SSM_CONV_CH = SSM_WIDTH + 2 * SSM_GROUPS * SSM_STATE
SSM_CHUNK = 128
CF_WIDTH = MIX_WIDTH
CF_CONV = 31
MOE_GROUPS = 4
EXPERTS_PER_GROUP = 4
N_EXPERTS = MOE_GROUPS * EXPERTS_PER_GROUP
TOP_K_IN_GROUP = 2
D_EXPERT = D_MODEL // 2
PLE_DIM = 256
OFF_Q = 0
OFF_K = OFF_Q + SB_WIDTH
OFF_V = OFF_K + SB_WIDTH
OFF_Z = OFF_V + SB_WIDTH
OFF_XBC = OFF_Z + SSM_WIDTH
OFF_DT = OFF_XBC + SSM_CONV_CH
OFF_GLU = OFF_DT + SSM_HEADS
OFF_GATE = OFF_GLU + 2 * CF_WIDTH
IN_COLS = OFF_GATE + N_BRANCH * D_MODEL

kernel_name = 'hybrid_stickbreak_ssd_conformer_hmoe_step'


def rmsnorm(x, g):
    xf = x.astype(F32)
    y = xf * lax.rsqrt(jnp.mean(xf * xf, axis=-1, keepdims=True) + EPS) * g.astype(F32)
    return y.astype(x.dtype)


def layernorm(x, g, b):
    xf = x.astype(F32)
    mu = jnp.mean(xf, axis=-1, keepdims=True)
    xc = xf - mu
    var = jnp.mean(xc * xc, axis=-1, keepdims=True)
    return (xc * lax.rsqrt(var + EPS) * g.astype(F32) + b.astype(F32)).astype(x.dtype)


def causal_dwconv(buf, x, w, b):
    xp = jnp.concatenate([buf.astype(x.dtype), x], axis=1)
    out = lax.conv_general_dilated(xp, w.astype(x.dtype)[:, None, :], window_strides=(1,), padding='VALID',
                                   dimension_numbers=('NWC', 'WIO', 'NWC'), feature_group_count=x.shape[-1])
    return out + b.astype(x.dtype), xp[:, xp.shape[1] - (w.shape[0] - 1):]


def stick_breaking_attention(q, k, v, q_start, sb_bias):
    b, lq, h, dh = q.shape
    lk = k.shape[1]
    qb = Q_BLOCK if lq % Q_BLOCK == 0 else lq
    nb = lq // qb
    scale = dh ** -0.5
    key_pos = jnp.arange(lk)
    q_blocks = q.reshape(b, nb, qb, h, dh).swapaxes(0, 1)
    starts = q_start + jnp.arange(nb) * qb
    bias = sb_bias.astype(F32)[None, :, None, None]

    def block(args):
        qblk, s0 = args
        z = jnp.einsum('bqhd,bkhd->bhqk', qblk, k, preferred_element_type=F32) * scale + bias
        qpos = s0 + jnp.arange(qb)
        mask = key_pos[None, :] < qpos[:, None]
        log_1mb = jnp.where(mask, jax.nn.log_sigmoid(-z), 0.0)
        after = lax.cumsum(log_1mb, axis=3, reverse=True) - log_1mb
        w = jnp.where(mask, jnp.exp(jax.nn.log_sigmoid(z) + after), 0.0)
        return jnp.einsum('bhqk,bkhd->bqhd', w.astype(v.dtype), v)

    out = lax.map(block, (q_blocks, starts))
    return out.swapaxes(0, 1).reshape(b, lq, h, dh)


def ssd_scan(x, dt, a, bm, cm, h0):
    b, L, H, P = x.shape
    G, N = bm.shape[2], bm.shape[3]
    E = H // G
    q = SSM_CHUNK if L % SSM_CHUNK == 0 else L
    nc = L // q

    def chunks(t):
        return t.reshape((b, nc, q) + t.shape[2:]).swapaxes(0, 1)

    xs = chunks(x.astype(F32).reshape(b, L, G, E, P))
    dts = chunks(dt.astype(F32).reshape(b, L, G, E))
    bs = chunks(bm.astype(F32))
    cs = chunks(cm.astype(F32))
    a_ge = a.reshape(G, E)
    causal = jnp.tril(jnp.ones((q, q), dtype=bool))[None, :, :, None, None]

    def step(h, inp):
        xc, dtc, bc, cc = inp
        cum = jnp.cumsum(dtc * a_ge, axis=1)
        seg = cum[:, :, None] - cum[:, None, :]
        decay = jnp.exp(jnp.where(causal, seg, -jnp.inf))
        cb = jnp.einsum('btgn,bsgn->btsg', cc, bc)
        w = cb[..., None] * decay * dtc[:, None]
        y = jnp.einsum('btsge,bsgep->btgep', w, xc)
        y = y + jnp.einsum('btgn,bgepn->btgep', cc, h) * jnp.exp(cum)[..., None]
        w_end = jnp.exp(cum[:, -1:] - cum) * dtc
        h = h * jnp.exp(cum[:, -1])[..., None, None] + jnp.einsum('bsge,bsgep,bsgn->bgepn', w_end, xc, bc)
        return h, y

    h_end, ys = lax.scan(step, h0.astype(F32).reshape(b, G, E, P, N), (xs, dts, bs, cs))
    return ys.swapaxes(0, 1).reshape(b, L, H, P), h_end.reshape(b, H, P, N)


def mamba2_branch(z, xbc, dt_raw, conv_buf, h0, conv_w, conv_b, dt_bias, a_log, d_skip, norm_g):
    b, L, _ = xbc.shape
    xbc_c, conv_new = causal_dwconv(conv_buf, xbc, conv_w, conv_b)
    xbc_c = jax.nn.silu(xbc_c)
    xs = xbc_c[..., :SSM_WIDTH].reshape(b, L, SSM_HEADS, SSM_HEAD_DIM)
    bm = xbc_c[..., SSM_WIDTH:SSM_WIDTH + SSM_GROUPS * SSM_STATE].reshape(b, L, SSM_GROUPS, SSM_STATE)
    cm = xbc_c[..., SSM_WIDTH + SSM_GROUPS * SSM_STATE:].reshape(b, L, SSM_GROUPS, SSM_STATE)
    dt = jax.nn.softplus(dt_raw.astype(F32) + dt_bias.astype(F32))
    a = -jnp.exp(a_log.astype(F32))
    y, h_new = ssd_scan(xs, dt, a, bm, cm, h0)
    y = y + d_skip.astype(F32)[:, None] * xs.astype(F32)
    y = y.reshape(b, L, SSM_WIDTH).astype(z.dtype) * jax.nn.silu(z)
    return rmsnorm(y, norm_g), conv_new, h_new.astype(z.dtype)


def conformer_branch(glu_in, conv_buf, conv_w, conv_b, ln_g, ln_b):
    u = glu_in[..., :CF_WIDTH] * jax.nn.sigmoid(glu_in[..., CF_WIDTH:])
    c, conv_new = causal_dwconv(conv_buf, u, conv_w, conv_b)
    return jax.nn.silu(layernorm(c, ln_g, ln_b)), conv_new


def hier_moe(h, w_rg, b_rg, w_re, b_re, w_gate, w_up, w_down):
    b, L, d = h.shape
    t = h.reshape(-1, d)
    g_logits = (t @ w_rg).astype(F32) + b_rg.astype(F32)
    g_prob = jax.nn.softmax(g_logits, axis=-1)
    g_idx = jnp.argmax(g_logits, axis=-1)
    g_w = jnp.take_along_axis(g_prob, g_idx[:, None], axis=-1)
    e_logits = jnp.einsum('td,dge->tge', t, w_re).astype(F32) + b_re.astype(F32)
    e_sel = jnp.take_along_axis(e_logits, g_idx[:, None, None], axis=1)[:, 0]
    top_v, top_i = lax.top_k(e_sel, TOP_K_IN_GROUP)
    top_w = jax.nn.softmax(top_v, axis=-1) * g_w
    expert_id = g_idx[:, None] * EXPERTS_PER_GROUP + top_i
    combine = jnp.sum(jax.nn.one_hot(expert_id, N_EXPERTS, dtype=F32) * top_w[..., None], axis=1)
    hg = jnp.einsum('td,edf->tef', t, w_gate)
    hu = jnp.einsum('td,edf->tef', t, w_up)
    act = jax.nn.silu(hg) * hu * combine[..., None].astype(t.dtype)
    return jnp.einsum('tef,efd->td', act, w_down).reshape(b, L, d)


def layer_forward(x, p_i, k_past, v_past, ssm_conv_buf, ssm_h0, cf_buf, q_start,
                  norm_mix, w_in, sb_bias, ssm_conv_w, ssm_conv_b, dt_bias, a_log, d_skip, ssm_norm,
                  cf_conv_w, cf_conv_b, cf_ln_g, cf_ln_b, w_branch, w_out,
                  norm_ffn, w_router_group, b_router_group, w_router_expert, b_router_expert,
                  w_expert_gate, w_expert_up, w_expert_down, norm_ple, w_ple_gate, w_ple_proj):
    b, L, _ = x.shape
    h = rmsnorm(x, norm_mix)
    proj = h @ w_in
    q = proj[..., OFF_Q:OFF_K].reshape(b, L, SB_HEADS, SB_HEAD_DIM)
    k_new = proj[..., OFF_K:OFF_V].reshape(b, L, SB_HEADS, SB_HEAD_DIM)
    v_new = proj[..., OFF_V:OFF_Z].reshape(b, L, SB_HEADS, SB_HEAD_DIM)
    if k_past is None:
        k_all, v_all = k_new, v_new
    else:
        k_all = jnp.concatenate([k_past.astype(k_new.dtype), k_new], axis=1)
        v_all = jnp.concatenate([v_past.astype(v_new.dtype), v_new], axis=1)
    a_out = stick_breaking_attention(q, k_all, v_all, q_start, sb_bias).reshape(b, L, SB_WIDTH)
    b_out, ssm_conv_new, ssm_h_new = mamba2_branch(
        proj[..., OFF_Z:OFF_XBC], proj[..., OFF_XBC:OFF_DT], proj[..., OFF_DT:OFF_GLU],
        ssm_conv_buf, ssm_h0, ssm_conv_w, ssm_conv_b, dt_bias, a_log, d_skip, ssm_norm)
    c_out, cf_new = conformer_branch(proj[..., OFF_GLU:OFF_GATE], cf_buf, cf_conv_w, cf_conv_b, cf_ln_g, cf_ln_b)
    branches = jnp.stack([a_out.astype(x.dtype), b_out.astype(x.dtype), c_out.astype(x.dtype)], axis=0)
    u = jnp.einsum('nblw,nwd->nbld', branches, w_branch)
    gates = jax.nn.sigmoid(proj[..., OFF_GATE:].reshape(b, L, N_BRANCH, D_MODEL))
    merged = jnp.einsum('blnd,nbld->bld', gates, u)
    x = x + merged @ w_out
    x = x + hier_moe(rmsnorm(x, norm_ffn), w_router_group, b_router_group, w_router_expert,
                     b_router_expert, w_expert_gate, w_expert_up, w_expert_down)
    ple_gate = jax.nn.sigmoid(rmsnorm(x, norm_ple) @ w_ple_gate)
    x = x + ple_gate * (p_i.astype(x.dtype) @ w_ple_proj)
    return x, k_new, v_new, ssm_h_new, ssm_conv_new, cf_new


def setup_inputs(seed: int = 0) -> dict:
    key = jax.random.key(seed)
    ks = iter(jax.random.split(key, 64))

    def nrm(shape, scale):
        return jax.random.normal(next(ks), shape, F32) * scale

    def gain(shape):
        return 1.0 + nrm(shape, 0.01)

    n_pages = PAST_LEN // PAGE_SIZE
    n_pool = (DEC_BATCH * n_pages * 5) // 4
    x_prompt = nrm((BATCH, SEQ, D_MODEL), 1.0)
    x_sample = nrm((DEC_BATCH, DEC_SEQ, D_MODEL), 1.0)
    p_prompt = nrm((DEPTH, BATCH, SEQ, PLE_DIM), 1.0)
    p_sample = nrm((DEPTH, DEC_BATCH, DEC_SEQ, PLE_DIM), 1.0)
    cache_k = nrm((DEPTH, n_pool, PAGE_SIZE, SB_HEADS, SB_HEAD_DIM), 1.0)
    cache_v = nrm((DEPTH, n_pool, PAGE_SIZE, SB_HEADS, SB_HEAD_DIM), 1.0)
    page_table = jax.random.permutation(next(ks), n_pool)[:DEC_BATCH * n_pages].reshape(DEC_BATCH, n_pages).astype(jnp.int32)
    state_ssm = nrm((DEPTH, DEC_BATCH, SSM_HEADS, SSM_HEAD_DIM, SSM_STATE), 0.1)
    state_ssm_conv = nrm((DEPTH, DEC_BATCH, SSM_CONV - 1, SSM_CONV_CH), 1.0)
    state_cf_conv = nrm((DEPTH, DEC_BATCH, CF_CONV - 1, CF_WIDTH), 0.5)
    sb_bias = jax.random.uniform(next(ks), (DEPTH, SB_HEADS), F32, -7.0, -4.0)
    dt0 = jnp.exp(jax.random.uniform(next(ks), (DEPTH, SSM_HEADS), F32, math.log(1e-3), math.log(1e-1)))
    dt_bias = dt0 + jnp.log(-jnp.expm1(-dt0))
    a_log = jnp.log(jax.random.uniform(next(ks), (DEPTH, SSM_HEADS), F32, 1.0, 16.0))
    return {
        'x_prompt': x_prompt, 'x_sample': x_sample, 'p_prompt': p_prompt, 'p_sample': p_sample,
        'cache_k': cache_k, 'cache_v': cache_v, 'page_table': page_table,
        'state_ssm': state_ssm, 'state_ssm_conv': state_ssm_conv, 'state_cf_conv': state_cf_conv,
        'norm_mix': gain((DEPTH, D_MODEL)),
        'w_in': nrm((DEPTH, D_MODEL, IN_COLS), D_MODEL ** -0.5),
        'sb_bias': sb_bias,
        'ssm_conv_w': nrm((DEPTH, SSM_CONV, SSM_CONV_CH), SSM_CONV ** -0.5),
        'ssm_conv_b': nrm((DEPTH, SSM_CONV_CH), 0.01),
        'dt_bias': dt_bias,
        'a_log': a_log,
        'd_skip': gain((DEPTH, SSM_HEADS)),
        'ssm_norm': gain((DEPTH, SSM_WIDTH)),
        'cf_conv_w': nrm((DEPTH, CF_CONV, CF_WIDTH), CF_CONV ** -0.5),
        'cf_conv_b': nrm((DEPTH, CF_WIDTH), 0.01),
        'cf_ln_g': gain((DEPTH, CF_WIDTH)),
        'cf_ln_b': nrm((DEPTH, CF_WIDTH), 0.01),
        'w_branch': nrm((DEPTH, N_BRANCH, MIX_WIDTH, D_MODEL), MIX_WIDTH ** -0.5),
        'w_out': nrm((DEPTH, D_MODEL, D_MODEL), D_MODEL ** -0.5),
        'norm_ffn': gain((DEPTH, D_MODEL)),
        'w_router_group': nrm((DEPTH, D_MODEL, MOE_GROUPS), D_MODEL ** -0.5),
        'b_router_group': nrm((DEPTH, MOE_GROUPS), 0.01),
        'w_router_expert': nrm((DEPTH, D_MODEL, MOE_GROUPS, EXPERTS_PER_GROUP), D_MODEL ** -0.5),
        'b_router_expert': nrm((DEPTH, MOE_GROUPS, EXPERTS_PER_GROUP), 0.01),
        'w_expert_gate': nrm((DEPTH, N_EXPERTS, D_MODEL, D_EXPERT), D_MODEL ** -0.5),
        'w_expert_up': nrm((DEPTH, N_EXPERTS, D_MODEL, D_EXPERT), D_MODEL ** -0.5),
        'w_expert_down': nrm((DEPTH, N_EXPERTS, D_EXPERT, D_MODEL), D_EXPERT ** -0.5),
        'norm_ple': gain((DEPTH, D_MODEL)),
        'w_ple_gate': nrm((DEPTH, D_MODEL, D_MODEL), D_MODEL ** -0.5),
        'w_ple_proj': nrm((DEPTH, PLE_DIM, D_MODEL), PLE_DIM ** -0.5),
        'norm_final': gain((D_MODEL,)),
    }


def reference(x_prompt, x_sample, p_prompt, p_sample, cache_k, cache_v, page_table,
              state_ssm, state_ssm_conv, state_cf_conv,
              norm_mix, w_in, sb_bias, ssm_conv_w, ssm_conv_b, dt_bias, a_log, d_skip, ssm_norm,
              cf_conv_w, cf_conv_b, cf_ln_g, cf_ln_b, w_branch, w_out,
              norm_ffn, w_router_group, b_router_group, w_router_expert, b_router_expert,
              w_expert_gate, w_expert_up, w_expert_down, norm_ple, w_ple_gate, w_ple_proj, norm_final):
    layer_params = [dict(norm_mix=norm_mix[i], w_in=w_in[i], sb_bias=sb_bias[i],
                         ssm_conv_w=ssm_conv_w[i], ssm_conv_b=ssm_conv_b[i],
                         dt_bias=dt_bias[i], a_log=a_log[i], d_skip=d_skip[i], ssm_norm=ssm_norm[i],
                         cf_conv_w=cf_conv_w[i], cf_conv_b=cf_conv_b[i], cf_ln_g=cf_ln_g[i], cf_ln_b=cf_ln_b[i],
                         w_branch=w_branch[i], w_out=w_out[i], norm_ffn=norm_ffn[i],
                         w_router_group=w_router_group[i], b_router_group=b_router_group[i],
                         w_router_expert=w_router_expert[i], b_router_expert=b_router_expert[i],
                         w_expert_gate=w_expert_gate[i], w_expert_up=w_expert_up[i], w_expert_down=w_expert_down[i],
                         norm_ple=norm_ple[i], w_ple_gate=w_ple_gate[i], w_ple_proj=w_ple_proj[i])
                    for i in range(DEPTH)]

    bp = x_prompt.shape[0]
    xp = x_prompt
    kp, vp, sp, scp, cfp = [], [], [], [], []
    for i in range(DEPTH):
        zeros_ssm_conv = jnp.zeros((bp, SSM_CONV - 1, SSM_CONV_CH), xp.dtype)
        zeros_h = jnp.zeros((bp, SSM_HEADS, SSM_HEAD_DIM, SSM_STATE), F32)
        zeros_cf = jnp.zeros((bp, CF_CONV - 1, CF_WIDTH), xp.dtype)
        xp, k_new, v_new, h_new, sc_new, cf_new = layer_forward(
            xp, p_prompt[i], None, None, zeros_ssm_conv, zeros_h, zeros_cf, 0, **layer_params[i])
        kp.append(k_new); vp.append(v_new); sp.append(h_new); scp.append(sc_new); cfp.append(cf_new)
    y_prompt = rmsnorm(xp, norm_final)

    bs = x_sample.shape[0]
    n_pages = page_table.shape[1]
    past = n_pages * PAGE_SIZE
    xs = x_sample
    ks_, vs_, ss_, scs_, cfs_ = [], [], [], [], []
    for i in range(DEPTH):
        k_past = cache_k[i][page_table].reshape(bs, past, SB_HEADS, SB_HEAD_DIM)
        v_past = cache_v[i][page_table].reshape(bs, past, SB_HEADS, SB_HEAD_DIM)
        xs, k_new, v_new, h_new, sc_new, cf_new = layer_forward(
            xs, p_sample[i], k_past, v_past, state_ssm_conv[i], state_ssm[i], state_cf_conv[i], past,
            **layer_params[i])
        ks_.append(k_new); vs_.append(v_new); ss_.append(h_new); scs_.append(sc_new); cfs_.append(cf_new)
    y_sample = rmsnorm(xs, norm_final)

    return (y_prompt, y_sample,
            jnp.stack(kp), jnp.stack(vp), jnp.stack(sp), jnp.stack(scp), jnp.stack(cfp),
            jnp.stack(ks_), jnp.stack(vs_), jnp.stack(ss_), jnp.stack(scs_), jnp.stack(cfs_))
```

```python
import functools

import jax
import jax.numpy as jnp
from jax import lax
from jax.experimental import pallas as pl
from jax.experimental.pallas import tpu as pltpu

F32 = jnp.float32
BF16 = jnp.bfloat16
EPS = 1e-6

D_MODEL = 1024
MIX_WIDTH = 512
N_BRANCH = 3
SB_HEADS = 8
SB_HEAD_DIM = 64
SSM_HEADS = 8
SSM_HEAD_DIM = 64
SSM_GROUPS = 2
SSM_STATE = 128
SSM_CONV = 4
SSM_CONV_CH = MIX_WIDTH + 2 * SSM_GROUPS * SSM_STATE
SSM_CHUNK = 128
CF_CONV = 31
MOE_GROUPS = 4
EXPERTS_PER_GROUP = 4
N_EXPERTS = MOE_GROUPS * EXPERTS_PER_GROUP
D_EXPERT = 512
PLE_DIM = 256
PAGE_SIZE = 128
OFF_DT = 3 * MIX_WIDTH + MIX_WIDTH + SSM_CONV_CH
OFF_GLU = OFF_DT + SSM_HEADS
OFF_GATE = OFF_GLU + 2 * MIX_WIDTH

LANES = 128
SUBLANES = 8
VMEM_LIMIT = 56 * 1024 * 1024

TOKEN_TILE = 256
MOE_TOKEN_TILE = 832
ATTN_TILE = 256
CF_TILE = 256
DEC_PAGES_PER_STEP = 8


def _cparams(sem):
    return pltpu.CompilerParams(dimension_semantics=sem, vmem_limit_bytes=VMEM_LIMIT)


def _rmsnorm(x, g):
    return x * lax.rsqrt(jnp.mean(x * x, axis=-1, keepdims=True) + EPS) * g


def _sigmoid(x):
    return 1.0 / (1.0 + jnp.exp(-x))


def _silu(x):
    return x * _sigmoid(x)


def _softplus(x):
    return jnp.maximum(x, 0.0) + jnp.log(1.0 + jnp.exp(-jnp.abs(x)))


def _dot(a, b):
    return jnp.dot(a, b, preferred_element_type=F32)


def _dot_nt(a, b):
    return lax.dot_general(a, b, (((1,), (1,)), ((), ())), preferred_element_type=F32)


def _dot_tn(a, b):
    return lax.dot_general(a, b, (((0,), (0,)), ((), ())), preferred_element_type=F32)


def _split3(x):
    hi = x.astype(BF16)
    r = x - hi.astype(F32)
    mid = r.astype(BF16)
    lo = (r - mid.astype(F32)).astype(BF16)
    return hi, mid, lo


def _dot_exact_lhs(mat_bf16, x):
    hi, mid, lo = _split3(x)
    return _dot(mat_bf16, hi) + _dot(mat_bf16, mid) + _dot(mat_bf16, lo)


IN_PROJ_WIDTHS = [MIX_WIDTH, MIX_WIDTH, MIX_WIDTH, MIX_WIDTH, MIX_WIDTH, MIX_WIDTH,
                  SSM_CONV_CH, 2 * MIX_WIDTH, LANES]
IN_PROJ_DTYPES = [BF16, F32, F32, BF16, BF16, F32, F32, F32, F32]


def _in_proj_kernel(x_ref, g_ref, w_ref, q_ref, k_ref, v_ref, kb_ref, vb_ref,
                    z_ref, xbc_ref, glu_ref, dt_ref, *, transposed_kv):
    h = _rmsnorm(x_ref[...], g_ref[...]).astype(BF16)
    w = MIX_WIDTH
    if transposed_kv:
        q = _dot_nt(h, w_ref[0:w, :])
        kv = _dot_nt(w_ref[w:3 * w, :], h)
        k = kv[0:w]
        v = kv[w:2 * w]
        rest = _dot_nt(h, w_ref[3 * w:, :])
    else:
        p = _dot_nt(h, w_ref[...])
        q = p[:, 0:w]
        k = p[:, w:2 * w]
        v = p[:, 2 * w:3 * w]
        rest = p[:, 3 * w:]
    q_ref[...] = (q * (SB_HEAD_DIM ** -0.5)).astype(BF16)
    k_ref[...] = k
    v_ref[...] = v
    kb_ref[...] = k.astype(BF16)
    vb_ref[...] = v.astype(BF16)
    z_ref[...] = rest[:, 0:w]
    xbc_ref[...] = rest[:, w:3 * w]
    glu_ref[...] = rest[:, 3 * w:5 * w]
    dt_ref[...] = rest[:, 5 * w:5 * w + LANES]


def _in_proj_prompt(x, g, w_main_t, batch, seq):
    tm = TOKEN_TILE
    nt = seq // tm
    n = w_main_t.shape[0]
    row = lambda width: pl.BlockSpec((tm, width), lambda b, j: (b * nt + j, 0))
    par = lambda shape: pl.BlockSpec(shape, lambda b, j: (0,) * len(shape))
    out_specs = [row(wd) for wd in IN_PROJ_WIDTHS]
    out_shape = [jax.ShapeDtypeStruct((batch * seq, wd), dt)
                 for wd, dt in zip(IN_PROJ_WIDTHS, IN_PROJ_DTYPES)]
    for idx in (1, 2):
        out_specs[idx] = pl.BlockSpec((None, MIX_WIDTH, tm), lambda b, j: (b, 0, j))
        out_shape[idx] = jax.ShapeDtypeStruct((batch, MIX_WIDTH, seq), F32)
    for idx in (3, 4):
        out_specs[idx] = pl.BlockSpec((None, None, MIX_WIDTH, tm), lambda b, j: (b, j, 0, 0))
        out_shape[idx] = jax.ShapeDtypeStruct((batch, nt, MIX_WIDTH, tm), BF16)
    return pl.pallas_call(
        functools.partial(_in_proj_kernel, transposed_kv=True),
        grid=(batch, nt),
        in_specs=[row(D_MODEL), par((1, D_MODEL)), par((n, D_MODEL))],
        out_specs=out_specs,
        out_shape=out_shape,
        compiler_params=_cparams(("parallel", "parallel")),
        name="in_proj_prompt",
    )(x, g, w_main_t)


def _in_proj_rows(x, g, w_main_t, row_start, rows):
    tm = TOKEN_TILE
    first = row_start // tm
    n = w_main_t.shape[0]
    par = lambda shape: pl.BlockSpec(shape, lambda i: (0,) * len(shape))
    return pl.pallas_call(
        functools.partial(_in_proj_kernel, transposed_kv=False),
        grid=(rows // tm,),
        in_specs=[pl.BlockSpec((tm, D_MODEL), lambda i: (i + first, 0)),
                  par((1, D_MODEL)), par((n, D_MODEL))],
        out_specs=[pl.BlockSpec((tm, wd), lambda i: (i, 0)) for wd in IN_PROJ_WIDTHS],
        out_shape=[jax.ShapeDtypeStruct((rows, wd), dt)
                   for wd, dt in zip(IN_PROJ_WIDTHS, IN_PROJ_DTYPES)],
        compiler_params=_cparams(("parallel",)),
        name="in_proj_rows",
    )(x, g, w_main_t)


def _strict_upper(n):
    r = lax.broadcasted_iota(jnp.int32, (n, n), 0)
    c = lax.broadcasted_iota(jnp.int32, (n, n), 1)
    return jnp.where(r > c, 1.0, 0.0).astype(BF16)


def _sb_block(z, v_blk, upper, carry, mask, v_channel_major=False):
    sp = _softplus(z)
    if mask is not None:
        sp = jnp.where(mask, sp, 0.0)
    hi = sp.astype(BF16)
    lo = (sp - hi.astype(F32)).astype(BF16)
    after = _dot(hi, upper) + _dot(lo, upper)
    w = jnp.exp(z - sp - after - carry)
    if mask is not None:
        w = jnp.where(mask, w, 0.0)
    pv = _dot_nt if v_channel_major else _dot
    return pv(w.astype(BF16), v_blk), jnp.sum(sp, axis=1, keepdims=True)


def _attn_prompt_kernel(bias_ref, q_ref, k_ref, v_ref, o_ref, *, seq, tile):
    hp = pl.program_id(1)
    upper = _strict_upper(tile)
    lane = lax.broadcasted_iota(jnp.int32, (tile, LANES), 1)
    r = lax.broadcasted_iota(jnp.int32, (tile, tile), 0)
    c = lax.broadcasted_iota(jnp.int32, (tile, tile), 1)
    diag_mask = c < r
    n_q = seq // tile

    def q_tile(qi, _):
        q0 = pl.multiple_of(qi * tile, tile)
        q = q_ref[pl.ds(q0, tile), :]
        out = jnp.zeros((tile, LANES), F32)
        for hh in range(2):
            in_head = (lane >= hh * SB_HEAD_DIM) & (lane < (hh + 1) * SB_HEAD_DIM)
            qm = jnp.where(in_head, q, jnp.zeros_like(q))
            bias = bias_ref[hp * 2 + hh]
            z = _dot(qm, k_ref[qi]) + bias
            acc, carry = _sb_block(z, v_ref[qi], upper, 0.0, diag_mask, True)

            def k_tile(i, st):
                acc, carry = st
                kj = qi - 1 - i
                z = _dot(qm, k_ref[kj]) + bias
                contrib, rs = _sb_block(z, v_ref[kj], upper, carry, None, True)
                return acc + contrib, carry + rs

            acc, carry = lax.fori_loop(0, qi, k_tile, (acc, carry))
            out = jnp.where(in_head, acc, out)
        o_ref[pl.ds(q0, tile), :] = out
        return 0

    lax.fori_loop(0, n_q, q_tile, 0)


def _attn_prompt(q, kt, vt, sb_bias, batch, seq, out_rows):
    tile = kt.shape[-1]
    nt = seq // tile
    blk = pl.BlockSpec((seq, LANES), lambda b, hp: (b, hp))
    kv_blk = pl.BlockSpec((None, nt, LANES, tile), lambda b, hp: (b, 0, hp, 0))
    return pl.pallas_call(
        functools.partial(_attn_prompt_kernel, seq=seq, tile=tile),
        grid=(batch, SB_HEADS // 2),
        in_specs=[pl.BlockSpec(memory_space=pltpu.SMEM), blk, kv_blk, kv_blk],
        out_specs=blk,
        out_shape=jax.ShapeDtypeStruct((out_rows, MIX_WIDTH), F32),
        compiler_params=_cparams(("parallel", "parallel")),
        name="attn_prompt",
    )(sb_bias, q, kt, vt)


def _attn_decode_kernel(pt_ref, bias_ref, q_ref, kn_ref, vn_ref, *rest, pages, dec_seq):
    k_refs = rest[:pages]
    v_refs = rest[pages:2 * pages]
    o_ref = rest[2 * pages]
    acc_ref, carry_ref = rest[2 * pages + 1:]
    step = pl.program_id(1)
    rows = SB_HEADS * dec_seq
    width = MIX_WIDTH
    upper = _strict_upper(PAGE_SIZE)
    r_w = lax.broadcasted_iota(jnp.int32, (rows, width), 0)
    c_w = lax.broadcasted_iota(jnp.int32, (rows, width), 1)
    head_mask = (r_w // dec_seq) == (c_w // SB_HEAD_DIM)
    q_rep = jnp.concatenate([q_ref[0].astype(F32)] * SB_HEADS, axis=0)
    q_bd = jnp.where(head_mask, q_rep, 0.0).astype(BF16)
    r_k = lax.broadcasted_iota(jnp.int32, (rows, PAGE_SIZE), 0)
    c_k = lax.broadcasted_iota(jnp.int32, (rows, PAGE_SIZE), 1)
    bias = jnp.zeros((rows, PAGE_SIZE), F32)
    for h in range(SB_HEADS):
        bias = jnp.where(r_k // dec_seq == h, bias_ref[h], bias)

    @pl.when(step == 0)
    def _():
        pad = jnp.zeros((PAGE_SIZE - dec_seq, width), F32)
        kn = jnp.concatenate([kn_ref[0].astype(F32), pad], axis=0).astype(BF16)
        vn = jnp.concatenate([vn_ref[0].astype(F32), pad], axis=0).astype(BF16)
        z = _dot_nt(q_bd, kn) + bias
        mask = c_k < (r_k % dec_seq)
        contrib, rs = _sb_block(z, vn, upper, 0.0, mask)
        acc_ref[...] = contrib
        carry_ref[...] = jnp.broadcast_to(rs, (rows, PAGE_SIZE))

    zs = [_dot(q_bd, k_refs[j][...].astype(BF16)) + bias for j in range(pages)]
    sps = [_softplus(z) for z in zs]
    carry = carry_ref[...]
    acc = acc_ref[...]
    for j in range(pages):
        sp = sps[j]
        hi = sp.astype(BF16)
        lo = (sp - hi.astype(F32)).astype(BF16)
        after = _dot(hi, upper) + _dot(lo, upper)
        w = jnp.exp(zs[j] - sp - after - carry)
        acc = acc + _dot_nt(w.astype(BF16), v_refs[j][...].astype(BF16))
        carry = carry + jnp.sum(sp, axis=1, keepdims=True)
    acc_ref[...] = acc
    carry_ref[...] = carry

    @pl.when(step == pl.num_programs(1) - 1)
    def _():
        a = jnp.where(head_mask, acc, 0.0)
        out = a[0:dec_seq]
        for h in range(1, SB_HEADS):
            out = out + a[h * dec_seq:(h + 1) * dec_seq]
        o_ref[0] = out


def _attn_decode(q, k_new, v_new, cache_k, cache_v, page_table, sb_bias, layer):
    bsz, dec_seq, width = q.shape
    n_pages = page_table.shape[1]
    pages = DEC_PAGES_PER_STEP
    assert n_pages % pages == 0 and dec_seq == SUBLANES
    steps = n_pages // pages
    rows = SB_HEADS * dec_seq

    def page_spec(j):
        def imap(b, s, pt):
            return (layer, pt[b * n_pages + (n_pages - 1 - (s * pages + j))], 0, 0)
        return pl.BlockSpec((None, None, width, PAGE_SIZE), imap)

    seq_spec = pl.BlockSpec((1, dec_seq, width), lambda b, s, pt: (b, 0, 0))
    grid_spec = pltpu.PrefetchScalarGridSpec(
        num_scalar_prefetch=1,
        grid=(bsz, steps),
        in_specs=[pl.BlockSpec(memory_space=pltpu.SMEM), seq_spec, seq_spec, seq_spec]
        + [page_spec(j) for j in range(pages)] * 2,
        out_specs=seq_spec,
        scratch_shapes=[pltpu.VMEM((rows, width), F32), pltpu.VMEM((rows, PAGE_SIZE), F32)],
    )
    return pl.pallas_call(
        functools.partial(_attn_decode_kernel, pages=pages, dec_seq=dec_seq),
        grid_spec=grid_spec,
        out_shape=jax.ShapeDtypeStruct((bsz, dec_seq, width), F32),
        compiler_params=_cparams(("parallel", "arbitrary")),
        name="attn_decode",
    )(page_table.reshape(-1), sb_bias, q, k_new, v_new,
      *([cache_k] * pages), *([cache_v] * pages))


def _expand_heads(cols, width):
    m = cols.shape[0]
    lane = lax.broadcasted_iota(jnp.int32, (m, width), 1)
    out = jnp.zeros((m, width), F32)
    for h in range(width // SSM_HEAD_DIM):
        out = jnp.where(lane // SSM_HEAD_DIM == h, cols[:, h:h + 1], out)
    return out


def _ssd_kernel(xbc_ref, dt_ref, z_ref, buf_ref, h0_ref, cw_ref, cb_ref, dtb_ref, a_ref,
                dskip_ref, ng_ref, y_ref, conv_ref, h_ref, xp_ref, hs_ref, *, chunk, valid):
    c = pl.program_id(1)
    q = chunk
    halo = SUBLANES

    @pl.when(c == 0)
    def _():
        xp_ref[0:halo, :] = buf_ref[0]
        hs_ref[...] = h0_ref[0]

    xp_ref[halo:halo + q, :] = xbc_ref[...]
    acc = cb_ref[...] + cw_ref[SSM_CONV - 1:SSM_CONV, :] * xp_ref[halo:halo + q, :]
    for j in range(SSM_CONV - 1):
        off = halo - (SSM_CONV - 1) + j
        acc = acc + cw_ref[j:j + 1, :] * xp_ref[off:off + q, :]
    last = min(valid, q)
    tail = xp_ref[last:last + halo, :]
    conv_ref[0] = tail
    xp_ref[0:halo, :] = tail
    xc = _silu(acc)
    xs = xc[:, 0:MIX_WIDTH]
    gw = SSM_GROUPS * SSM_STATE
    bm = xc[:, MIX_WIDTH:MIX_WIDTH + gw]
    cm = xc[:, MIX_WIDTH + gw:MIX_WIDTH + 2 * gw]

    dt = _softplus(dt_ref[...] + dtb_ref[...])
    if valid < q:
        row = lax.broadcasted_iota(jnp.int32, (q, LANES), 0)
        dt = jnp.where(row < valid, dt, 0.0)
    da = dt * a_ref[...]
    r = lax.broadcasted_iota(jnp.int32, (q, q), 0)
    s = lax.broadcasted_iota(jnp.int32, (q, q), 1)
    causal = s <= r
    tril = jnp.where(causal, 1.0, 0.0).astype(BF16)
    cum = _dot_exact_lhs(tril, da)
    cum_t = jnp.transpose(cum)
    dt_t = jnp.transpose(dt)
    cum_last = cum[q - 1:q, :]
    w_end = jnp.exp(cum_last - cum) * dt
    e_cum = jnp.exp(cum)

    hpg = SSM_HEADS // SSM_GROUPS
    gwid = hpg * SSM_HEAD_DIM
    lane_g = lax.broadcasted_iota(jnp.int32, (q, gwid), 1)
    xs_b = xs.astype(BF16)
    e_cum_x = _expand_heads(e_cum, MIX_WIDTH)
    w_end_x = _expand_heads(w_end, MIX_WIDTH)
    xw = (xs * w_end_x).astype(BF16)
    y_parts = []
    for g in range(SSM_GROUPS):
        b_g = bm[:, g * SSM_STATE:(g + 1) * SSM_STATE].astype(BF16)
        c_g = cm[:, g * SSM_STATE:(g + 1) * SSM_STATE].astype(BF16)
        cb = _dot_nt(c_g, b_g)
        xs_g = xs_b[:, g * gwid:(g + 1) * gwid]
        y_g = jnp.zeros((q, gwid), F32)
        for e in range(hpg):
            hd = g * hpg + e
            seg = cum[:, hd:hd + 1] - cum_t[hd:hd + 1, :]
            decay = jnp.where(causal, jnp.exp(jnp.where(causal, seg, 0.0)), 0.0)
            w = (cb * decay * dt_t[hd:hd + 1, :]).astype(BF16)
            y_g = jnp.where(lane_g // SSM_HEAD_DIM == e, _dot(w, xs_g), y_g)
        h_g = hs_ref[g * hpg:(g + 1) * hpg].reshape(gwid, SSM_STATE)
        y_prev = _dot_nt(c_g, h_g.astype(BF16))
        y_parts.append(y_g + y_prev * e_cum_x[:, g * gwid:(g + 1) * gwid])
        upd = _dot_tn(xw[:, g * gwid:(g + 1) * gwid], b_g)
        for e in range(hpg):
            hd = g * hpg + e
            scale = jnp.exp(cum_last[:, hd:hd + 1])
            hs_ref[hd] = hs_ref[hd] * scale + upd[e * SSM_HEAD_DIM:(e + 1) * SSM_HEAD_DIM, :]
    y = jnp.concatenate(y_parts, axis=1) + dskip_ref[...] * xs
    y = y * _silu(z_ref[...])
    y_ref[...] = _rmsnorm(y, ng_ref[...])
    h_ref[0] = hs_ref[...]


def _ssd(xbc, dt_raw, z, conv_buf, h0, conv_w, conv_b, dt_bias, a_neg, d_skip_row, norm_g,
         batch, seq, valid, out_rows=None):
    q = SSM_CHUNK
    nc = seq // q
    row = lambda width: pl.BlockSpec((q, width), lambda b, c: (b * nc + c, 0))
    par = lambda shape: pl.BlockSpec(shape, lambda b, c: (0,) * len(shape))
    return pl.pallas_call(
        functools.partial(_ssd_kernel, chunk=q, valid=valid),
        grid=(batch, nc),
        in_specs=[row(SSM_CONV_CH), row(LANES), row(MIX_WIDTH),
                  pl.BlockSpec((1, SUBLANES, SSM_CONV_CH), lambda b, c: (b, 0, 0)),
                  pl.BlockSpec((1, SSM_HEADS, SSM_HEAD_DIM, SSM_STATE), lambda b, c: (b, 0, 0, 0)),
                  par((SSM_CONV, SSM_CONV_CH)), par((1, SSM_CONV_CH)), par((1, LANES)),
                  par((1, LANES)), par((1, MIX_WIDTH)), par((1, MIX_WIDTH))],
        out_specs=[row(MIX_WIDTH),
                   pl.BlockSpec((1, SUBLANES, SSM_CONV_CH), lambda b, c: (b, 0, 0)),
                   pl.BlockSpec((1, SSM_HEADS, SSM_HEAD_DIM, SSM_STATE), lambda b, c: (b, 0, 0, 0))],
        out_shape=[jax.ShapeDtypeStruct((out_rows or batch * seq, MIX_WIDTH), F32),
                   jax.ShapeDtypeStruct((batch, SUBLANES, SSM_CONV_CH), F32),
                   jax.ShapeDtypeStruct((batch, SSM_HEADS, SSM_HEAD_DIM, SSM_STATE), F32)],
        scratch_shapes=[pltpu.VMEM((q + 2 * SUBLANES, SSM_CONV_CH), F32),
                        pltpu.VMEM((SSM_HEADS, SSM_HEAD_DIM, SSM_STATE), F32)],
        compiler_params=_cparams(("parallel", "arbitrary")),
        name="ssd",
    )(xbc, dt_raw, z, conv_buf, h0, conv_w, conv_b, dt_bias, a_neg, d_skip_row, norm_g)


CF_HALO = 32


def _conformer_kernel(glu_ref, buf_ref, cw_ref, cb_ref, lg_ref, lb_ref, y_ref, new_ref,
                      xp_ref, *, tile):
    c = pl.program_id(1)

    @pl.when(c == 0)
    def _():
        xp_ref[0:CF_HALO, :] = buf_ref[0]

    glu = glu_ref[...]
    u = glu[:, 0:MIX_WIDTH] * _sigmoid(glu[:, MIX_WIDTH:2 * MIX_WIDTH])
    xp_ref[CF_HALO:CF_HALO + tile, :] = u
    acc = cb_ref[...] + cw_ref[CF_CONV - 1:CF_CONV, :] * u
    for j in range(CF_CONV - 1):
        off = CF_HALO - (CF_CONV - 1) + j
        acc = acc + cw_ref[j:j + 1, :] * xp_ref[off:off + tile, :]
    tail = xp_ref[tile:tile + CF_HALO, :]
    new_ref[0] = tail
    xp_ref[0:CF_HALO, :] = tail
    mu = jnp.mean(acc, axis=-1, keepdims=True)
    xc = acc - mu
    var = jnp.mean(xc * xc, axis=-1, keepdims=True)
    y_ref[...] = _silu(xc * lax.rsqrt(var + EPS) * lg_ref[...] + lb_ref[...])


def _conformer(glu, conv_buf, conv_w, conv_b, ln_g, ln_b, batch, seq, out_rows=None):
    tile = CF_TILE if seq % CF_TILE == 0 else seq
    nt = seq // tile
    par = lambda shape: pl.BlockSpec(shape, lambda b, c: (0,) * len(shape))
    buf_spec = pl.BlockSpec((1, CF_HALO, MIX_WIDTH), lambda b, c: (b, 0, 0))
    return pl.pallas_call(
        functools.partial(_conformer_kernel, tile=tile),
        grid=(batch, nt),
        in_specs=[pl.BlockSpec((tile, 2 * MIX_WIDTH), lambda b, c: (b * nt + c, 0)), buf_spec,
                  par((CF_CONV, MIX_WIDTH)), par((1, MIX_WIDTH)), par((1, MIX_WIDTH)),
                  par((1, MIX_WIDTH))],
        out_specs=[pl.BlockSpec((tile, MIX_WIDTH), lambda b, c: (b * nt + c, 0)), buf_spec],
        out_shape=[jax.ShapeDtypeStruct((out_rows or batch * seq, MIX_WIDTH), F32),
                   jax.ShapeDtypeStruct((batch, CF_HALO, MIX_WIDTH), F32)],
        scratch_shapes=[pltpu.VMEM((tile + CF_HALO, MIX_WIDTH), F32)],
        compiler_params=_cparams(("parallel", "arbitrary")),
        name="conformer",
    )(glu, conv_buf, conv_w, conv_b, ln_g, ln_b)


def _merge_kernel(x_ref, a_ref, b_ref, c_ref, g_ref, wg_ref, wb_ref, wo_ref, o_ref):
    x = x_ref[...]
    h = _rmsnorm(x, g_ref[...]).astype(BF16)
    merged = jnp.zeros(x.shape, F32)
    for n, br in enumerate((a_ref, b_ref, c_ref)):
        gate = _sigmoid(_dot_nt(h, wg_ref[n * D_MODEL:(n + 1) * D_MODEL, :]))
        merged = merged + gate * _dot(br[...].astype(BF16), wb_ref[n])
    o_ref[...] = x + _dot(merged.astype(BF16), wo_ref[...])


def _merge(x, a, b, c, g, w_gates, w_branch, w_out):
    t = x.shape[0]
    tm = TOKEN_TILE
    row = lambda width: pl.BlockSpec((tm, width), lambda i: (i, 0))
    par = lambda shape: pl.BlockSpec(shape, lambda i: (0,) * len(shape))
    return pl.pallas_call(
        _merge_kernel,
        grid=(t // tm,),
        in_specs=[row(D_MODEL), row(MIX_WIDTH), row(MIX_WIDTH), row(MIX_WIDTH),
                  par((1, D_MODEL)), par((N_BRANCH * D_MODEL, D_MODEL)),
                  par((N_BRANCH, MIX_WIDTH, D_MODEL)), par((D_MODEL, D_MODEL))],
        out_specs=row(D_MODEL),
        out_shape=jax.ShapeDtypeStruct((t, D_MODEL), F32),
        compiler_params=_cparams(("parallel",)),
        name="merge",
    )(x, a, b, c, g, w_gates, w_branch, w_out)


def _router_kernel(x_ref, g_ref, wr_ref, br_ref, t_ref, comb_ref):
    t = _rmsnorm(x_ref[...], g_ref[...])
    t_ref[...] = t.astype(BF16)
    logits = jnp.dot(t, wr_ref[...], preferred_element_type=F32,
                     precision=lax.Precision.HIGHEST) + br_ref[...]
    lane = lax.broadcasted_iota(jnp.int32, logits.shape, 1)
    neg = -jnp.inf
    is_group = (lane >= N_EXPERTS) & (lane < N_EXPERTS + MOE_GROUPS)
    gl = jnp.where(is_group, logits, neg)
    gmax = jnp.max(gl, axis=1, keepdims=True)
    g_idx = jnp.min(jnp.where(gl == gmax, lane, LANES), axis=1, keepdims=True) - N_EXPERTS
    g_w = 1.0 / jnp.sum(jnp.where(is_group, jnp.exp(logits - gmax), 0.0), axis=1, keepdims=True)
    in_group = (lane // EXPERTS_PER_GROUP) == g_idx
    el = jnp.where(in_group, logits, neg)
    v1 = jnp.max(el, axis=1, keepdims=True)
    i1 = jnp.min(jnp.where(el == v1, lane, LANES), axis=1, keepdims=True)
    el2 = jnp.where(lane == i1, neg, el)
    v2 = jnp.max(el2, axis=1, keepdims=True)
    i2 = jnp.min(jnp.where(el2 == v2, lane, LANES), axis=1, keepdims=True)
    e21 = jnp.exp(v2 - v1)
    p1 = 1.0 / (1.0 + e21)
    p2 = e21 * p1
    comb_ref[...] = jnp.where(lane == i1, p1 * g_w, 0.0) + jnp.where(lane == i2, p2 * g_w, 0.0)


def _router(x, g, w_router, b_router):
    t = x.shape[0]
    tm = TOKEN_TILE
    row = lambda width: pl.BlockSpec((tm, width), lambda i: (i, 0))
    par = lambda shape: pl.BlockSpec(shape, lambda i: (0,) * len(shape))
    return pl.pallas_call(
        _router_kernel,
        grid=(t // tm,),
        in_specs=[row(D_MODEL), par((1, D_MODEL)), par((D_MODEL, LANES)), par((1, LANES))],
        out_specs=[row(D_MODEL), row(LANES)],
        out_shape=[jax.ShapeDtypeStruct((t, D_MODEL), BF16),
                   jax.ShapeDtypeStruct((t, LANES), F32)],
        compiler_params=_cparams(("parallel",)),
        name="router",
    )(x, g, w_router, b_router)


def _experts_kernel(x_ref, t_ref, comb_ref, wg_ref, wu_ref, wd_ref, o_ref, acc_ref):
    e = pl.program_id(1)

    @pl.when(e == 0)
    def _():
        acc_ref[...] = x_ref[...]

    t = t_ref[...]
    comb = comb_ref[...]
    lane = lax.broadcasted_iota(jnp.int32, comb.shape, 1)
    w_e = jnp.sum(jnp.where(lane == e, comb, 0.0), axis=1, keepdims=True)
    hg = _dot(t, wg_ref[0].astype(BF16))
    hu = _dot(t, wu_ref[0].astype(BF16))
    act = (_silu(hg) * hu * w_e).astype(BF16)
    acc_ref[...] += _dot(act, wd_ref[0].astype(BF16))

    @pl.when(e == pl.num_programs(1) - 1)
    def _():
        o_ref[...] = acc_ref[...]


def _experts(x, t, comb, w_gate, w_up, w_down):
    n = x.shape[0]
    tm = MOE_TOKEN_TILE
    row = lambda width: pl.BlockSpec((tm, width), lambda i, e: (i, 0))
    return pl.pallas_call(
        _experts_kernel,
        grid=(n // tm, N_EXPERTS),
        in_specs=[row(D_MODEL), row(D_MODEL), row(LANES),
                  pl.BlockSpec((1, D_MODEL, D_EXPERT), lambda i, e: (e, 0, 0)),
                  pl.BlockSpec((1, D_MODEL, D_EXPERT), lambda i, e: (e, 0, 0)),
                  pl.BlockSpec((1, D_EXPERT, D_MODEL), lambda i, e: (e, 0, 0))],
        out_specs=row(D_MODEL),
        out_shape=jax.ShapeDtypeStruct((n, D_MODEL), F32),
        scratch_shapes=[pltpu.VMEM((tm, D_MODEL), F32)],
        compiler_params=_cparams(("parallel", "arbitrary")),
        name="experts",
    )(x, t, comb, w_gate, w_up, w_down)


def _ple_kernel(x_ref, p_ref, g_ref, wg_ref, wp_ref, gf_ref, o_ref, *maybe_y_ref):
    x = x_ref[...]
    h = _rmsnorm(x, g_ref[...]).astype(BF16)
    gate = _sigmoid(_dot(h, wg_ref[...]))
    out = x + gate * _dot(p_ref[...].astype(BF16), wp_ref[...])
    o_ref[...] = out
    if maybe_y_ref:
        maybe_y_ref[0][...] = _rmsnorm(out, gf_ref[...])


def _ple(x, p, g, w_gate, w_proj, g_final, final):
    t = x.shape[0]
    tm = TOKEN_TILE
    row = lambda width: pl.BlockSpec((tm, width), lambda i: (i, 0))
    par = lambda shape: pl.BlockSpec(shape, lambda i: (0,) * len(shape))
    n_out = 2 if final else 1
    return pl.pallas_call(
        _ple_kernel,
        grid=(t // tm,),
        in_specs=[row(D_MODEL), row(PLE_DIM), par((1, D_MODEL)), par((D_MODEL, D_MODEL)),
                  par((PLE_DIM, D_MODEL)), par((1, D_MODEL))],
        out_specs=[row(D_MODEL)] * n_out,
        out_shape=[jax.ShapeDtypeStruct((t, D_MODEL), F32)] * n_out,
        compiler_params=_cparams(("parallel",)),
        name="ple",
    )(x, p, g, w_gate, w_proj, g_final)


def _pad_rows(a, rows):
    return jnp.pad(a, ((0, 0), (rows - a.shape[1], 0), (0, 0)))


def _layer(i, x, p, cache_k, cache_v, page_table, state_ssm, state_ssm_conv, state_cf_conv,
           prm, dims, final):
    bp, sp_len, bs, ss_len = dims
    tp = bp * sp_len
    ts = bs * ss_len
    row = lambda v: v.reshape(1, -1)

    w_t = jnp.transpose(prm['w_in'][i])
    dt_rows = jnp.pad(w_t[OFF_DT:OFF_GLU], ((0, LANES - SSM_HEADS), (0, 0)))
    w_main_t = jnp.concatenate([w_t[:OFF_DT], w_t[OFF_GLU:OFF_GATE], dt_rows],
                               axis=0).astype(BF16)
    w_gates_t = w_t[OFF_GATE:].astype(BF16)
    g_mix = row(prm['norm_mix'][i])

    qb_p, kt_p, vt_p, ktb_p, vtb_p, z_p, xbc_p, glu_p, dt_p = _in_proj_prompt(
        x, g_mix, w_main_t, bp, sp_len)
    qb_s, k_s, v_s, kb_s, vb_s, z_s, xbc_s, glu_s, dt_s = _in_proj_rows(
        x, g_mix, w_main_t, tp, ts)
    put_sample = lambda full, part: lax.dynamic_update_slice(full, part, (tp, 0))

    a_out = _attn_prompt(qb_p, ktb_p, vtb_p, prm['sb_bias'][i], bp, sp_len, tp + ts)
    seq3 = lambda a: a.reshape(bs, ss_len, MIX_WIDTH)
    a_sample = _attn_decode(seq3(qb_s), seq3(kb_s), seq3(vb_s), cache_k, cache_v, page_table,
                            prm['sb_bias'][i], i)
    a_out = put_sample(a_out, a_sample.reshape(ts, MIX_WIDTH))

    pad_lanes = lambda vec, fill: jnp.pad(vec, (0, LANES - vec.shape[0]),
                                          constant_values=fill).reshape(1, LANES)
    ssd_prm = (prm['ssm_conv_w'][i], row(prm['ssm_conv_b'][i]), pad_lanes(prm['dt_bias'][i], 0.0),
               pad_lanes(-jnp.exp(prm['a_log'][i]), 0.0),
               row(jnp.repeat(prm['d_skip'][i], SSM_HEAD_DIM)), row(prm['ssm_norm'][i]))
    zeros_conv = jnp.zeros((bp, SUBLANES, SSM_CONV_CH), F32)
    zeros_h = jnp.zeros((bp, SSM_HEADS, SSM_HEAD_DIM, SSM_STATE), F32)
    b_out, sconv_p, h_p = _ssd(xbc_p, dt_p, z_p, zeros_conv, zeros_h, *ssd_prm,
                               batch=bp, seq=sp_len, valid=sp_len, out_rows=tp + ts)
    padseq = lambda a: jnp.pad(a.reshape(bs, ss_len, -1),
                               ((0, 0), (0, SSM_CHUNK - ss_len), (0, 0))).reshape(bs * SSM_CHUNK, -1)
    b_s, sconv_s, h_s = _ssd(padseq(xbc_s), padseq(dt_s), padseq(z_s),
                             _pad_rows(state_ssm_conv[i], SUBLANES), state_ssm[i], *ssd_prm,
                             batch=bs, seq=SSM_CHUNK, valid=ss_len)
    b_sample = b_s.reshape(bs, SSM_CHUNK, MIX_WIDTH)[:, :ss_len].reshape(ts, MIX_WIDTH)
    b_out = put_sample(b_out, b_sample)

    cf_prm = (prm['cf_conv_w'][i], row(prm['cf_conv_b'][i]), row(prm['cf_ln_g'][i]),
              row(prm['cf_ln_b'][i]))
    c_out, cf_p = _conformer(glu_p, jnp.zeros((bp, CF_HALO, MIX_WIDTH), F32), *cf_prm,
                             batch=bp, seq=sp_len, out_rows=tp + ts)
    c_sample, cf_s = _conformer(glu_s, _pad_rows(state_cf_conv[i], CF_HALO), *cf_prm,
                                batch=bs, seq=ss_len)
    c_out = put_sample(c_out, c_sample)

    x = _merge(x, a_out, b_out, c_out, g_mix, w_gates_t,
               prm['w_branch'][i].astype(BF16), prm['w_out'][i].astype(BF16))

    w_router = jnp.concatenate(
        [prm['w_router_expert'][i].reshape(D_MODEL, N_EXPERTS), prm['w_router_group'][i],
         jnp.zeros((D_MODEL, LANES - N_EXPERTS - MOE_GROUPS), F32)], axis=1)
    b_router = jnp.concatenate(
        [prm['b_router_expert'][i].reshape(N_EXPERTS), prm['b_router_group'][i],
         jnp.zeros((LANES - N_EXPERTS - MOE_GROUPS,), F32)]).reshape(1, LANES)
    t_norm, comb = _router(x, row(prm['norm_ffn'][i]), w_router, b_router)
    x = _experts(x, t_norm, comb, prm['w_expert_gate'][i], prm['w_expert_up'][i],
                 prm['w_expert_down'][i])

    outs = _ple(x, p, row(prm['norm_ple'][i]), prm['w_ple_gate'][i].astype(BF16),
                prm['w_ple_proj'][i].astype(BF16), row(prm['norm_final']), final)

    conv_tail = SSM_CONV - 1
    cf_tail = CF_CONV - 1
    states = dict(
        k_p=jnp.transpose(kt_p.reshape(bp, SB_HEADS, SB_HEAD_DIM, sp_len), (0, 3, 1, 2)),
        v_p=jnp.transpose(vt_p.reshape(bp, SB_HEADS, SB_HEAD_DIM, sp_len), (0, 3, 1, 2)),
        h_p=h_p, sconv_p=sconv_p[:, SUBLANES - conv_tail:], cf_p=cf_p[:, CF_HALO - cf_tail:],
        k_s=k_s.reshape(bs, ss_len, SB_HEADS, SB_HEAD_DIM),
        v_s=v_s.reshape(bs, ss_len, SB_HEADS, SB_HEAD_DIM),
        h_s=h_s, sconv_s=sconv_s[:, SUBLANES - conv_tail:], cf_s=cf_s[:, CF_HALO - cf_tail:])
    return outs, states


def kernel(x_prompt, x_sample, p_prompt, p_sample, cache_k, cache_v, page_table, state_ssm, state_ssm_conv, state_cf_conv, norm_mix, w_in, sb_bias, ssm_conv_w, ssm_conv_b, dt_bias, a_log, d_skip, ssm_norm, cf_conv_w, cf_conv_b, cf_ln_g, cf_ln_b, w_branch, w_out, norm_ffn, w_router_group, b_router_group, w_router_expert, b_router_expert, w_expert_gate, w_expert_up, w_expert_down, norm_ple, w_ple_gate, w_ple_proj, norm_final):
    prm = dict(norm_mix=norm_mix, w_in=w_in, sb_bias=sb_bias, ssm_conv_w=ssm_conv_w,
               ssm_conv_b=ssm_conv_b, dt_bias=dt_bias, a_log=a_log, d_skip=d_skip,
               ssm_norm=ssm_norm, cf_conv_w=cf_conv_w, cf_conv_b=cf_conv_b, cf_ln_g=cf_ln_g,
               cf_ln_b=cf_ln_b, w_branch=w_branch, w_out=w_out, norm_ffn=norm_ffn,
               w_router_group=w_router_group, b_router_group=b_router_group,
               w_router_expert=w_router_expert, b_router_expert=b_router_expert,
               w_expert_gate=w_expert_gate, w_expert_up=w_expert_up,
               w_expert_down=w_expert_down, norm_ple=norm_ple, w_ple_gate=w_ple_gate,
               w_ple_proj=w_ple_proj, norm_final=norm_final)
    depth = w_in.shape[0]
    bp, sp_len, _ = x_prompt.shape
    bs, ss_len, _ = x_sample.shape
    tp = bp * sp_len
    ts = bs * ss_len
    dims = (bp, sp_len, bs, ss_len)
    n_pool = cache_k.shape[1]
    ck = jnp.transpose(cache_k, (0, 1, 3, 4, 2)).reshape(depth, n_pool, MIX_WIDTH, PAGE_SIZE)
    cv = jnp.transpose(cache_v, (0, 1, 3, 4, 2)).reshape(depth, n_pool, MIX_WIDTH, PAGE_SIZE)

    x = jnp.concatenate([x_prompt.reshape(tp, D_MODEL), x_sample.reshape(ts, D_MODEL)], axis=0)
    per_layer = []
    for i in range(depth):
        p = jnp.concatenate([p_prompt[i].reshape(tp, PLE_DIM), p_sample[i].reshape(ts, PLE_DIM)],
                            axis=0)
        outs, st = _layer(i, x, p, ck, cv, page_table, state_ssm, state_ssm_conv, state_cf_conv,
                          prm, dims, final=(i == depth - 1))
        x = outs[0]
        per_layer.append(st)
    y = outs[1]
    stack = lambda name: jnp.stack([st[name] for st in per_layer])
    return (y[:tp].reshape(bp, sp_len, D_MODEL), y[tp:].reshape(bs, ss_len, D_MODEL),
            stack('k_p'), stack('v_p'), stack('h_p'), stack('sconv_p'), stack('cf_p'),
            stack('k_s'), stack('v_s'), stack('h_s'), stack('sconv_s'), stack('cf_s'))
```

```python
import functools

import jax
import jax.numpy as jnp
from jax import lax
from jax.experimental import pallas as pl
from jax.experimental.pallas import tpu as pltpu

F32 = jnp.float32
BF16 = jnp.bfloat16
EPS = 1e-6

D_MODEL = 1024
MIX_WIDTH = 512
N_BRANCH = 3
SB_HEADS = 8
SB_HEAD_DIM = 64
SSM_HEADS = 8
SSM_HEAD_DIM = 64
SSM_GROUPS = 2
SSM_STATE = 128
SSM_CONV = 4
SSM_CONV_CH = MIX_WIDTH + 2 * SSM_GROUPS * SSM_STATE
SSM_CHUNK = 128
CF_CONV = 31
MOE_GROUPS = 4
EXPERTS_PER_GROUP = 4
N_EXPERTS = MOE_GROUPS * EXPERTS_PER_GROUP
D_EXPERT = 512
PLE_DIM = 256
PAGE_SIZE = 128
OFF_DT = 3 * MIX_WIDTH + MIX_WIDTH + SSM_CONV_CH
OFF_GLU = OFF_DT + SSM_HEADS
OFF_GATE = OFF_GLU + 2 * MIX_WIDTH

LANES = 128
SUBLANES = 8
VMEM_LIMIT = 56 * 1024 * 1024

TOKEN_TILE = 256
MOE_TOKEN_TILE = 832
ATTN_TILE = 256
CF_TILE = 256
DEC_PAGES_PER_STEP = 16


def _cparams(sem):
    return pltpu.CompilerParams(dimension_semantics=sem, vmem_limit_bytes=VMEM_LIMIT)


def _rmsnorm(x, g):
    return x * lax.rsqrt(jnp.mean(x * x, axis=-1, keepdims=True) + EPS) * g


def _sigmoid(x):
    return 1.0 / (1.0 + jnp.exp(-x))


def _silu(x):
    return x * _sigmoid(x)


def _softplus(x):
    return jnp.maximum(x, 0.0) + jnp.log(1.0 + jnp.exp(-jnp.abs(x)))


LOG2E = 1.4426950408889634


def _softplus2(x):
    return jnp.maximum(x, 0.0) + jnp.log2(1.0 + jnp.exp2(-jnp.abs(x)))


def _dot(a, b):
    return jnp.dot(a, b, preferred_element_type=F32)


def _dot_nt(a, b):
    return lax.dot_general(a, b, (((1,), (1,)), ((), ())), preferred_element_type=F32)


def _dot_tn(a, b):
    return lax.dot_general(a, b, (((0,), (0,)), ((), ())), preferred_element_type=F32)


def _split3(x):
    hi = x.astype(BF16)
    r = x - hi.astype(F32)
    mid = r.astype(BF16)
    lo = (r - mid.astype(F32)).astype(BF16)
    return hi, mid, lo


def _dot_exact_lhs(mat_bf16, x):
    hi, mid, lo = _split3(x)
    return _dot(mat_bf16, hi) + _dot(mat_bf16, mid) + _dot(mat_bf16, lo)


IN_PROJ_WIDTHS = [MIX_WIDTH, MIX_WIDTH, MIX_WIDTH, MIX_WIDTH, MIX_WIDTH, MIX_WIDTH,
                  SSM_CONV_CH, 2 * MIX_WIDTH, LANES]
IN_PROJ_DTYPES = [BF16, F32, F32, BF16, BF16, F32, F32, F32, F32]


def _in_proj_kernel(x_ref, g_ref, w_ref, q_ref, k_ref, v_ref, kb_ref, vb_ref,
                    z_ref, xbc_ref, glu_ref, dt_ref, *, transposed_kv):
    h = _rmsnorm(x_ref[...], g_ref[...]).astype(BF16)
    w = MIX_WIDTH
    if transposed_kv:
        q = _dot_nt(h, w_ref[0:w, :])
        kv = _dot_nt(w_ref[w:3 * w, :], h)
        k = kv[0:w]
        v = kv[w:2 * w]
        rest = _dot_nt(h, w_ref[3 * w:, :])
    else:
        p = _dot_nt(h, w_ref[...])
        q = p[:, 0:w]
        k = p[:, w:2 * w]
        v = p[:, 2 * w:3 * w]
        rest = p[:, 3 * w:]
    q_ref[...] = (q * (SB_HEAD_DIM ** -0.5 * LOG2E)).astype(BF16)
    k_ref[...] = k
    v_ref[...] = v
    kb_ref[...] = k.astype(BF16)
    vb_ref[...] = v.astype(BF16)
    z_ref[...] = rest[:, 0:w]
    xbc_ref[...] = rest[:, w:3 * w]
    glu_ref[...] = rest[:, 3 * w:5 * w]
    dt_ref[...] = rest[:, 5 * w:5 * w + LANES]


def _in_proj_prompt(x, g, w_main_t, batch, seq):
    tm = TOKEN_TILE
    nt = seq // tm
    n = w_main_t.shape[0]
    row = lambda width: pl.BlockSpec((tm, width), lambda b, j: (b * nt + j, 0))
    par = lambda shape: pl.BlockSpec(shape, lambda b, j: (0,) * len(shape))
    out_specs = [row(wd) for wd in IN_PROJ_WIDTHS]
    out_shape = [jax.ShapeDtypeStruct((batch * seq, wd), dt)
                 for wd, dt in zip(IN_PROJ_WIDTHS, IN_PROJ_DTYPES)]
    for idx in (1, 2):
        out_specs[idx] = pl.BlockSpec((None, MIX_WIDTH, tm), lambda b, j: (b, 0, j))
        out_shape[idx] = jax.ShapeDtypeStruct((batch, MIX_WIDTH, seq), F32)
    for idx in (3, 4):
        out_specs[idx] = pl.BlockSpec((None, None, MIX_WIDTH, tm), lambda b, j: (b, j, 0, 0))
        out_shape[idx] = jax.ShapeDtypeStruct((batch, nt, MIX_WIDTH, tm), BF16)
    return pl.pallas_call(
        functools.partial(_in_proj_kernel, transposed_kv=True),
        grid=(batch, nt),
        in_specs=[row(D_MODEL), par((1, D_MODEL)), par((n, D_MODEL))],
        out_specs=out_specs,
        out_shape=out_shape,
        compiler_params=_cparams(("parallel", "parallel")),
        name="in_proj_prompt",
    )(x, g, w_main_t)


def _in_proj_rows(x, g, w_main_t, row_start, rows):
    tm = TOKEN_TILE
    first = row_start // tm
    n = w_main_t.shape[0]
    par = lambda shape: pl.BlockSpec(shape, lambda i: (0,) * len(shape))
    return pl.pallas_call(
        functools.partial(_in_proj_kernel, transposed_kv=False),
        grid=(rows // tm,),
        in_specs=[pl.BlockSpec((tm, D_MODEL), lambda i: (i + first, 0)),
                  par((1, D_MODEL)), par((n, D_MODEL))],
        out_specs=[pl.BlockSpec((tm, wd), lambda i: (i, 0)) for wd in IN_PROJ_WIDTHS],
        out_shape=[jax.ShapeDtypeStruct((rows, wd), dt)
                   for wd, dt in zip(IN_PROJ_WIDTHS, IN_PROJ_DTYPES)],
        compiler_params=_cparams(("parallel",)),
        name="in_proj_rows",
    )(x, g, w_main_t)


def _strict_upper(n):
    r = lax.broadcasted_iota(jnp.int32, (n, n), 0)
    c = lax.broadcasted_iota(jnp.int32, (n, n), 1)
    return jnp.where(r > c, 1.0, 0.0).astype(BF16)


def _sb_block(z, v_blk, upper, carry, mask, v_channel_major=False):
    sp = _softplus2(z)
    if mask is not None:
        sp = jnp.where(mask, sp, 0.0)
    after = _dot(sp.astype(BF16), upper)
    w = jnp.exp2(z - sp - after - carry)
    if mask is not None:
        w = jnp.where(mask, w, 0.0)
    pv = _dot_nt if v_channel_major else _dot
    return pv(w.astype(BF16), v_blk), jnp.sum(sp, axis=1, keepdims=True)


def _attn_prompt_kernel(bias_ref, q_ref, k_ref, v_ref, o_ref, acc_ref, carry_ref, *, seq, tile):
    hp = pl.program_id(1)
    n_q = seq // tile
    upper = _strict_upper(tile)
    lane = lax.broadcasted_iota(jnp.int32, (tile, LANES), 1)
    head0 = lane < SB_HEAD_DIM
    r = lax.broadcasted_iota(jnp.int32, (tile, tile), 0)
    c = lax.broadcasted_iota(jnp.int32, (tile, tile), 1)
    diag_mask = jnp.concatenate([c < r, c < r], axis=0)
    bias0 = bias_ref[hp * 2] * LOG2E
    bias1 = bias_ref[hp * 2 + 1] * LOG2E

    def visit(qis, kj, mask):
        zs, sps, afters, carries = [], [], [], []
        for qi in qis:
            q = q_ref[pl.ds(pl.multiple_of(qi * tile, tile), tile), :]
            zero = jnp.zeros_like(q)
            q2 = jnp.concatenate([jnp.where(head0, q, zero), jnp.where(head0, zero, q)], axis=0)
            d = _dot(q2, k_ref[kj])
            zs.append(jnp.concatenate([d[:tile] + bias0, d[tile:] + bias1], axis=0))
        for z in zs:
            sp = _softplus2(z)
            sps.append(sp if mask is None else jnp.where(mask, sp, 0.0))
        for sp in sps:
            afters.append(_dot(sp.astype(BF16), upper))
        contribs = []
        for qi, z, sp, after in zip(qis, zs, sps, afters):
            if mask is None:
                carry = carry_ref[qi]
                carries.append(carry)
                after = after + jnp.concatenate([carry] * (tile // LANES), axis=1)
            w = jnp.exp2(z - sp - after)
            if mask is not None:
                w = jnp.where(mask, w, 0.0)
            contribs.append(_dot_nt(w.astype(BF16), v_ref[kj]))
        for n, (qi, sp, contrib) in enumerate(zip(qis, sps, contribs)):
            row_sum = jnp.broadcast_to(jnp.sum(sp, axis=1, keepdims=True), (2 * tile, LANES))
            if mask is None:
                acc_ref[qi] += contrib
                carry_ref[qi] = carries[n] + row_sum
            else:
                acc_ref[qi] = contrib
                carry_ref[qi] = row_sum

    def key_tile(jj, _):
        kj = n_q - 1 - jj
        visit([kj], kj, diag_mask)

        def q_pair(i, _):
            visit([kj + 1 + 2 * i, kj + 2 + 2 * i], kj, None)
            return 0

        lax.fori_loop(0, jj // 2, q_pair, 0)

        @pl.when(jj % 2 == 1)
        def _():
            visit([n_q - 1], kj, None)

        return 0

    lax.fori_loop(0, n_q, key_tile, 0)

    def write(qi, _):
        acc = acc_ref[qi]
        o_ref[pl.ds(pl.multiple_of(qi * tile, tile), tile), :] = jnp.where(
            head0, acc[:tile], acc[tile:])
        return 0

    lax.fori_loop(0, n_q, write, 0)


def _attn_prompt(q, kt, vt, sb_bias, batch, seq, out_rows):
    tile = kt.shape[-1]
    nt = seq // tile
    blk = pl.BlockSpec((seq, LANES), lambda b, hp: (b, hp))
    kv_blk = pl.BlockSpec((None, nt, LANES, tile), lambda b, hp: (b, 0, hp, 0))
    return pl.pallas_call(
        functools.partial(_attn_prompt_kernel, seq=seq, tile=tile),
        grid=(batch, SB_HEADS // 2),
        in_specs=[pl.BlockSpec(memory_space=pltpu.SMEM), blk, kv_blk, kv_blk],
        out_specs=blk,
        out_shape=jax.ShapeDtypeStruct((out_rows, MIX_WIDTH), F32),
        scratch_shapes=[pltpu.VMEM((nt, 2 * tile, LANES), F32),
                        pltpu.VMEM((nt, 2 * tile, LANES), F32)],
        compiler_params=_cparams(("parallel", "parallel")),
        name="attn_prompt",
    )(sb_bias, q, kt, vt)


def _attn_decode_kernel(pt_ref, bias_ref, q_ref, kn_ref, vn_ref, *rest, pages, dec_seq):
    k_refs = rest[:pages]
    v_refs = rest[pages:2 * pages]
    o_ref = rest[2 * pages]
    acc_ref, carry_ref = rest[2 * pages + 1:]
    step = pl.program_id(1)
    rows = SB_HEADS * dec_seq
    width = MIX_WIDTH
    upper = _strict_upper(PAGE_SIZE)
    r_w = lax.broadcasted_iota(jnp.int32, (rows, width), 0)
    c_w = lax.broadcasted_iota(jnp.int32, (rows, width), 1)
    head_mask = (r_w // dec_seq) == (c_w // SB_HEAD_DIM)
    q_rep = jnp.concatenate([q_ref[0].astype(F32)] * SB_HEADS, axis=0)
    q_bd = jnp.where(head_mask, q_rep, 0.0).astype(BF16)
    r_k = lax.broadcasted_iota(jnp.int32, (rows, PAGE_SIZE), 0)
    c_k = lax.broadcasted_iota(jnp.int32, (rows, PAGE_SIZE), 1)
    bias = jnp.zeros((rows, PAGE_SIZE), F32)
    for h in range(SB_HEADS):
        bias = jnp.where(r_k // dec_seq == h, bias_ref[h] * LOG2E, bias)

    @pl.when(step == 0)
    def _():
        pad = jnp.zeros((PAGE_SIZE - dec_seq, width), F32)
        kn = jnp.concatenate([kn_ref[0].astype(F32), pad], axis=0).astype(BF16)
        vn = jnp.concatenate([vn_ref[0].astype(F32), pad], axis=0).astype(BF16)
        z = _dot_nt(q_bd, kn) + bias
        mask = c_k < (r_k % dec_seq)
        contrib, rs = _sb_block(z, vn, upper, 0.0, mask)
        acc_ref[...] = contrib
        carry_ref[...] = jnp.broadcast_to(rs, (rows, PAGE_SIZE))

    k_all = jnp.concatenate([k_refs[j][...].astype(BF16) for j in range(pages)], axis=1)
    z = _dot(q_bd, k_all)
    z = jnp.concatenate([z[:, j * PAGE_SIZE:(j + 1) * PAGE_SIZE] + bias for j in range(pages)],
                        axis=0)
    sp = _softplus2(z)
    after = _dot(sp.astype(BF16), upper)
    row_sum = jnp.sum(sp, axis=1, keepdims=True)
    carry = carry_ref[...]
    carries = []
    for j in range(pages):
        carries.append(carry)
        carry = carry + row_sum[j * rows:(j + 1) * rows]
    w = jnp.exp2(z - sp - after - jnp.concatenate(carries, axis=0)).astype(BF16)
    w_all = jnp.concatenate([w[j * rows:(j + 1) * rows] for j in range(pages)], axis=1)
    v_all = jnp.concatenate([v_refs[j][...].astype(BF16) for j in range(pages)], axis=1)
    acc = acc_ref[...] + _dot_nt(w_all, v_all)
    acc_ref[...] = acc
    carry_ref[...] = carry

    @pl.when(step == pl.num_programs(1) - 1)
    def _():
        a = jnp.where(head_mask, acc, 0.0)
        out = a[0:dec_seq]
        for h in range(1, SB_HEADS):
            out = out + a[h * dec_seq:(h + 1) * dec_seq]
        o_ref[0] = out


def _attn_decode(q, k_new, v_new, cache_k, cache_v, page_table, sb_bias, layer):
    bsz, dec_seq, width = q.shape
    n_pages = page_table.shape[1]
    pages = DEC_PAGES_PER_STEP
    assert n_pages % pages == 0 and dec_seq == SUBLANES
    steps = n_pages // pages
    rows = SB_HEADS * dec_seq

    def page_spec(j):
        def imap(b, s, pt):
            return (layer, pt[b * n_pages + (n_pages - 1 - (s * pages + j))], 0, 0)
        return pl.BlockSpec((None, None, width, PAGE_SIZE), imap)

    seq_spec = pl.BlockSpec((1, dec_seq, width), lambda b, s, pt: (b, 0, 0))
    grid_spec = pltpu.PrefetchScalarGridSpec(
        num_scalar_prefetch=1,
        grid=(bsz, steps),
        in_specs=[pl.BlockSpec(memory_space=pltpu.SMEM), seq_spec, seq_spec, seq_spec]
        + [page_spec(j) for j in range(pages)] * 2,
        out_specs=seq_spec,
        scratch_shapes=[pltpu.VMEM((rows, width), F32), pltpu.VMEM((rows, PAGE_SIZE), F32)],
    )
    return pl.pallas_call(
        functools.partial(_attn_decode_kernel, pages=pages, dec_seq=dec_seq),
        grid_spec=grid_spec,
        out_shape=jax.ShapeDtypeStruct((bsz, dec_seq, width), F32),
        compiler_params=_cparams(("parallel", "arbitrary")),
        name="attn_decode",
    )(page_table.reshape(-1), sb_bias, q, k_new, v_new,
      *([cache_k] * pages), *([cache_v] * pages))


def _expand_heads(cols, width):
    m = cols.shape[0]
    lane = lax.broadcasted_iota(jnp.int32, (m, width), 1)
    out = jnp.zeros((m, width), F32)
    for h in range(width // SSM_HEAD_DIM):
        out = jnp.where(lane // SSM_HEAD_DIM == h, cols[:, h:h + 1], out)
    return out


def _ssd_kernel(xbc_ref, dt_ref, z_ref, buf_ref, h0_ref, cw_ref, cb_ref, dtb_ref, a_ref,
                dskip_ref, ng_ref, y_ref, conv_ref, h_ref, xp_ref, hs_ref, *, chunk, valid):
    c = pl.program_id(1)
    q = chunk
    halo = SUBLANES

    @pl.when(c == 0)
    def _():
        xp_ref[0:halo, :] = buf_ref[0]
        hs_ref[...] = h0_ref[0]

    xp_ref[halo:halo + q, :] = xbc_ref[...]
    acc = cb_ref[...] + cw_ref[SSM_CONV - 1:SSM_CONV, :] * xp_ref[halo:halo + q, :]
    for j in range(SSM_CONV - 1):
        off = halo - (SSM_CONV - 1) + j
        acc = acc + cw_ref[j:j + 1, :] * xp_ref[off:off + q, :]
    last = min(valid, q)
    tail = xp_ref[last:last + halo, :]
    conv_ref[0] = tail
    xp_ref[0:halo, :] = tail
    xc = _silu(acc)
    xs = xc[:, 0:MIX_WIDTH]
    gw = SSM_GROUPS * SSM_STATE
    bm = xc[:, MIX_WIDTH:MIX_WIDTH + gw]
    cm = xc[:, MIX_WIDTH + gw:MIX_WIDTH + 2 * gw]

    dt = _softplus(dt_ref[...] + dtb_ref[...])
    if valid < q:
        row = lax.broadcasted_iota(jnp.int32, (q, LANES), 0)
        dt = jnp.where(row < valid, dt, 0.0)
    da = dt * a_ref[...]
    r = lax.broadcasted_iota(jnp.int32, (q, q), 0)
    s = lax.broadcasted_iota(jnp.int32, (q, q), 1)
    causal = s <= r
    tril = jnp.where(causal, 1.0, 0.0).astype(BF16)
    cum = _dot_exact_lhs(tril, da)
    cum_t = jnp.transpose(cum)
    dt_t = jnp.transpose(dt)
    cum_last = cum[q - 1:q, :]
    w_end = jnp.exp(cum_last - cum) * dt
    e_cum = jnp.exp(cum)

    hpg = SSM_HEADS // SSM_GROUPS
    gwid = hpg * SSM_HEAD_DIM
    lane_g = lax.broadcasted_iota(jnp.int32, (q, gwid), 1)
    xs_b = xs.astype(BF16)
    e_cum_x = _expand_heads(e_cum, MIX_WIDTH)
    w_end_x = _expand_heads(w_end, MIX_WIDTH)
    xw = (xs * w_end_x).astype(BF16)
    y_parts = []
    for g in range(SSM_GROUPS):
        b_g = bm[:, g * SSM_STATE:(g + 1) * SSM_STATE].astype(BF16)
        c_g = cm[:, g * SSM_STATE:(g + 1) * SSM_STATE].astype(BF16)
        cb = _dot_nt(c_g, b_g)
        xs_g = xs_b[:, g * gwid:(g + 1) * gwid]
        y_g = jnp.zeros((q, gwid), F32)
        for e in range(hpg):
            hd = g * hpg + e
            seg = cum[:, hd:hd + 1] - cum_t[hd:hd + 1, :]
            decay = jnp.where(causal, jnp.exp(jnp.where(causal, seg, 0.0)), 0.0)
            w = (cb * decay * dt_t[hd:hd + 1, :]).astype(BF16)
            y_g = jnp.where(lane_g // SSM_HEAD_DIM == e, _dot(w, xs_g), y_g)
        h_g = hs_ref[g * hpg:(g + 1) * hpg].reshape(gwid, SSM_STATE)
        y_prev = _dot_nt(c_g, h_g.astype(BF16))
        y_parts.append(y_g + y_prev * e_cum_x[:, g * gwid:(g + 1) * gwid])
        upd = _dot_tn(xw[:, g * gwid:(g + 1) * gwid], b_g)
        for e in range(hpg):
            hd = g * hpg + e
            scale = jnp.exp(cum_last[:, hd:hd + 1])
            hs_ref[hd] = hs_ref[hd] * scale + upd[e * SSM_HEAD_DIM:(e + 1) * SSM_HEAD_DIM, :]
    y = jnp.concatenate(y_parts, axis=1) + dskip_ref[...] * xs
    y = y * _silu(z_ref[...])
    y_ref[...] = _rmsnorm(y, ng_ref[...])
    h_ref[0] = hs_ref[...]


def _ssd(xbc, dt_raw, z, conv_buf, h0, conv_w, conv_b, dt_bias, a_neg, d_skip_row, norm_g,
         batch, seq, valid, out_rows=None):
    q = SSM_CHUNK
    nc = seq // q
    row = lambda width: pl.BlockSpec((q, width), lambda b, c: (b * nc + c, 0))
    par = lambda shape: pl.BlockSpec(shape, lambda b, c: (0,) * len(shape))
    return pl.pallas_call(
        functools.partial(_ssd_kernel, chunk=q, valid=valid),
        grid=(batch, nc),
        in_specs=[row(SSM_CONV_CH), row(LANES), row(MIX_WIDTH),
                  pl.BlockSpec((1, SUBLANES, SSM_CONV_CH), lambda b, c: (b, 0, 0)),
                  pl.BlockSpec((1, SSM_HEADS, SSM_HEAD_DIM, SSM_STATE), lambda b, c: (b, 0, 0, 0)),
                  par((SSM_CONV, SSM_CONV_CH)), par((1, SSM_CONV_CH)), par((1, LANES)),
                  par((1, LANES)), par((1, MIX_WIDTH)), par((1, MIX_WIDTH))],
        out_specs=[row(MIX_WIDTH),
                   pl.BlockSpec((1, SUBLANES, SSM_CONV_CH), lambda b, c: (b, 0, 0)),
                   pl.BlockSpec((1, SSM_HEADS, SSM_HEAD_DIM, SSM_STATE), lambda b, c: (b, 0, 0, 0))],
        out_shape=[jax.ShapeDtypeStruct((out_rows or batch * seq, MIX_WIDTH), F32),
                   jax.ShapeDtypeStruct((batch, SUBLANES, SSM_CONV_CH), F32),
                   jax.ShapeDtypeStruct((batch, SSM_HEADS, SSM_HEAD_DIM, SSM_STATE), F32)],
        scratch_shapes=[pltpu.VMEM((q + 2 * SUBLANES, SSM_CONV_CH), F32),
                        pltpu.VMEM((SSM_HEADS, SSM_HEAD_DIM, SSM_STATE), F32)],
        compiler_params=_cparams(("parallel", "arbitrary")),
        name="ssd",
    )(xbc, dt_raw, z, conv_buf, h0, conv_w, conv_b, dt_bias, a_neg, d_skip_row, norm_g)


CF_HALO = 32


def _conformer_kernel(glu_ref, buf_ref, cw_ref, cb_ref, lg_ref, lb_ref, y_ref, new_ref,
                      xp_ref, *, tile):
    c = pl.program_id(1)

    @pl.when(c == 0)
    def _():
        xp_ref[0:CF_HALO, :] = buf_ref[0]

    glu = glu_ref[...]
    u = glu[:, 0:MIX_WIDTH] * _sigmoid(glu[:, MIX_WIDTH:2 * MIX_WIDTH])
    xp_ref[CF_HALO:CF_HALO + tile, :] = u
    acc = cb_ref[...] + cw_ref[CF_CONV - 1:CF_CONV, :] * u
    for j in range(CF_CONV - 1):
        off = CF_HALO - (CF_CONV - 1) + j
        acc = acc + cw_ref[j:j + 1, :] * xp_ref[off:off + tile, :]
    tail = xp_ref[tile:tile + CF_HALO, :]
    new_ref[0] = tail
    xp_ref[0:CF_HALO, :] = tail
    mu = jnp.mean(acc, axis=-1, keepdims=True)
    xc = acc - mu
    var = jnp.mean(xc * xc, axis=-1, keepdims=True)
    y_ref[...] = _silu(xc * lax.rsqrt(var + EPS) * lg_ref[...] + lb_ref[...])


def _conformer(glu, conv_buf, conv_w, conv_b, ln_g, ln_b, batch, seq, out_rows=None):
    tile = CF_TILE if seq % CF_TILE == 0 else seq
    nt = seq // tile
    par = lambda shape: pl.BlockSpec(shape, lambda b, c: (0,) * len(shape))
    buf_spec = pl.BlockSpec((1, CF_HALO, MIX_WIDTH), lambda b, c: (b, 0, 0))
    return pl.pallas_call(
        functools.partial(_conformer_kernel, tile=tile),
        grid=(batch, nt),
        in_specs=[pl.BlockSpec((tile, 2 * MIX_WIDTH), lambda b, c: (b * nt + c, 0)), buf_spec,
                  par((CF_CONV, MIX_WIDTH)), par((1, MIX_WIDTH)), par((1, MIX_WIDTH)),
                  par((1, MIX_WIDTH))],
        out_specs=[pl.BlockSpec((tile, MIX_WIDTH), lambda b, c: (b * nt + c, 0)), buf_spec],
        out_shape=[jax.ShapeDtypeStruct((out_rows or batch * seq, MIX_WIDTH), F32),
                   jax.ShapeDtypeStruct((batch, CF_HALO, MIX_WIDTH), F32)],
        scratch_shapes=[pltpu.VMEM((tile + CF_HALO, MIX_WIDTH), F32)],
        compiler_params=_cparams(("parallel", "arbitrary")),
        name="conformer",
    )(glu, conv_buf, conv_w, conv_b, ln_g, ln_b)


def _merge_kernel(x_ref, a_ref, b_ref, c_ref, g_ref, wg_ref, wb_ref, wo_ref, o_ref):
    x = x_ref[...]
    h = _rmsnorm(x, g_ref[...]).astype(BF16)
    merged = jnp.zeros(x.shape, F32)
    for n, br in enumerate((a_ref, b_ref, c_ref)):
        gate = _sigmoid(_dot_nt(h, wg_ref[n * D_MODEL:(n + 1) * D_MODEL, :]))
        merged = merged + gate * _dot(br[...].astype(BF16), wb_ref[n])
    o_ref[...] = x + _dot(merged.astype(BF16), wo_ref[...])


def _merge(x, a, b, c, g, w_gates, w_branch, w_out):
    t = x.shape[0]
    tm = TOKEN_TILE
    row = lambda width: pl.BlockSpec((tm, width), lambda i: (i, 0))
    par = lambda shape: pl.BlockSpec(shape, lambda i: (0,) * len(shape))
    return pl.pallas_call(
        _merge_kernel,
        grid=(t // tm,),
        in_specs=[row(D_MODEL), row(MIX_WIDTH), row(MIX_WIDTH), row(MIX_WIDTH),
                  par((1, D_MODEL)), par((N_BRANCH * D_MODEL, D_MODEL)),
                  par((N_BRANCH, MIX_WIDTH, D_MODEL)), par((D_MODEL, D_MODEL))],
        out_specs=row(D_MODEL),
        out_shape=jax.ShapeDtypeStruct((t, D_MODEL), F32),
        compiler_params=_cparams(("parallel",)),
        name="merge",
    )(x, a, b, c, g, w_gates, w_branch, w_out)


def _router_kernel(x_ref, g_ref, wr_ref, br_ref, t_ref, comb_ref):
    t = _rmsnorm(x_ref[...], g_ref[...])
    t_ref[...] = t.astype(BF16)
    logits = jnp.dot(t, wr_ref[...], preferred_element_type=F32,
                     precision=lax.Precision.HIGHEST) + br_ref[...]
    lane = lax.broadcasted_iota(jnp.int32, logits.shape, 1)
    neg = -jnp.inf
    is_group = (lane >= N_EXPERTS) & (lane < N_EXPERTS + MOE_GROUPS)
    gl = jnp.where(is_group, logits, neg)
    gmax = jnp.max(gl, axis=1, keepdims=True)
    g_idx = jnp.min(jnp.where(gl == gmax, lane, LANES), axis=1, keepdims=True) - N_EXPERTS
    g_w = 1.0 / jnp.sum(jnp.where(is_group, jnp.exp(logits - gmax), 0.0), axis=1, keepdims=True)
    in_group = (lane // EXPERTS_PER_GROUP) == g_idx
    el = jnp.where(in_group, logits, neg)
    v1 = jnp.max(el, axis=1, keepdims=True)
    i1 = jnp.min(jnp.where(el == v1, lane, LANES), axis=1, keepdims=True)
    el2 = jnp.where(lane == i1, neg, el)
    v2 = jnp.max(el2, axis=1, keepdims=True)
    i2 = jnp.min(jnp.where(el2 == v2, lane, LANES), axis=1, keepdims=True)
    e21 = jnp.exp(v2 - v1)
    p1 = 1.0 / (1.0 + e21)
    p2 = e21 * p1
    comb_ref[...] = jnp.where(lane == i1, p1 * g_w, 0.0) + jnp.where(lane == i2, p2 * g_w, 0.0)


def _router(x, g, w_router, b_router):
    t = x.shape[0]
    tm = TOKEN_TILE
    row = lambda width: pl.BlockSpec((tm, width), lambda i: (i, 0))
    par = lambda shape: pl.BlockSpec(shape, lambda i: (0,) * len(shape))
    return pl.pallas_call(
        _router_kernel,
        grid=(t // tm,),
        in_specs=[row(D_MODEL), par((1, D_MODEL)), par((D_MODEL, LANES)), par((1, LANES))],
        out_specs=[row(D_MODEL), row(LANES)],
        out_shape=[jax.ShapeDtypeStruct((t, D_MODEL), BF16),
                   jax.ShapeDtypeStruct((t, LANES), F32)],
        compiler_params=_cparams(("parallel",)),
        name="router",
    )(x, g, w_router, b_router)


def _experts_kernel(x_ref, t_ref, comb_ref, wg_ref, wu_ref, wd_ref, o_ref, acc_ref):
    e = pl.program_id(1)

    @pl.when(e == 0)
    def _():
        acc_ref[...] = x_ref[...]

    t = t_ref[...]
    comb = comb_ref[...]
    lane = lax.broadcasted_iota(jnp.int32, comb.shape, 1)
    w_e = jnp.sum(jnp.where(lane == e, comb, 0.0), axis=1, keepdims=True)
    hg = _dot(t, wg_ref[0].astype(BF16))
    hu = _dot(t, wu_ref[0].astype(BF16))
    act = (_silu(hg) * hu * w_e).astype(BF16)
    acc_ref[...] += _dot(act, wd_ref[0].astype(BF16))

    @pl.when(e == pl.num_programs(1) - 1)
    def _():
        o_ref[...] = acc_ref[...]


def _experts(x, t, comb, w_gate, w_up, w_down):
    n = x.shape[0]
    tm = MOE_TOKEN_TILE
    row = lambda width: pl.BlockSpec((tm, width), lambda i, e: (i, 0))
    return pl.pallas_call(
        _experts_kernel,
        grid=(n // tm, N_EXPERTS),
        in_specs=[row(D_MODEL), row(D_MODEL), row(LANES),
                  pl.BlockSpec((1, D_MODEL, D_EXPERT), lambda i, e: (e, 0, 0)),
                  pl.BlockSpec((1, D_MODEL, D_EXPERT), lambda i, e: (e, 0, 0)),
                  pl.BlockSpec((1, D_EXPERT, D_MODEL), lambda i, e: (e, 0, 0))],
        out_specs=row(D_MODEL),
        out_shape=jax.ShapeDtypeStruct((n, D_MODEL), F32),
        scratch_shapes=[pltpu.VMEM((tm, D_MODEL), F32)],
        compiler_params=_cparams(("parallel", "arbitrary")),
        name="experts",
    )(x, t, comb, w_gate, w_up, w_down)


def _ple_kernel(x_ref, p_ref, g_ref, wg_ref, wp_ref, gf_ref, o_ref, *maybe_y_ref):
    x = x_ref[...]
    h = _rmsnorm(x, g_ref[...]).astype(BF16)
    gate = _sigmoid(_dot(h, wg_ref[...]))
    out = x + gate * _dot(p_ref[...].astype(BF16), wp_ref[...])
    o_ref[...] = out
    if maybe_y_ref:
        maybe_y_ref[0][...] = _rmsnorm(out, gf_ref[...])


def _ple(x, p, g, w_gate, w_proj, g_final, final):
    t = x.shape[0]
    tm = TOKEN_TILE
    row = lambda width: pl.BlockSpec((tm, width), lambda i: (i, 0))
    par = lambda shape: pl.BlockSpec(shape, lambda i: (0,) * len(shape))
    n_out = 2 if final else 1
    return pl.pallas_call(
        _ple_kernel,
        grid=(t // tm,),
        in_specs=[row(D_MODEL), row(PLE_DIM), par((1, D_MODEL)), par((D_MODEL, D_MODEL)),
                  par((PLE_DIM, D_MODEL)), par((1, D_MODEL))],
        out_specs=[row(D_MODEL)] * n_out,
        out_shape=[jax.ShapeDtypeStruct((t, D_MODEL), F32)] * n_out,
        compiler_params=_cparams(("parallel",)),
        name="ple",
    )(x, p, g, w_gate, w_proj, g_final)


def _pad_rows(a, rows):
    return jnp.pad(a, ((0, 0), (rows - a.shape[1], 0), (0, 0)))


def _layer(i, x, p, cache_k, cache_v, page_table, state_ssm, state_ssm_conv, state_cf_conv,
           prm, dims, final):
    bp, sp_len, bs, ss_len = dims
    tp = bp * sp_len
    ts = bs * ss_len
    row = lambda v: v.reshape(1, -1)

    w_t = jnp.transpose(prm['w_in'][i])
    dt_rows = jnp.pad(w_t[OFF_DT:OFF_GLU], ((0, LANES - SSM_HEADS), (0, 0)))
    w_main_t = jnp.concatenate([w_t[:OFF_DT], w_t[OFF_GLU:OFF_GATE], dt_rows],
                               axis=0).astype(BF16)
    w_gates_t = w_t[OFF_GATE:].astype(BF16)
    g_mix = row(prm['norm_mix'][i])

    qb_p, kt_p, vt_p, ktb_p, vtb_p, z_p, xbc_p, glu_p, dt_p = _in_proj_prompt(
        x, g_mix, w_main_t, bp, sp_len)
    qb_s, k_s, v_s, kb_s, vb_s, z_s, xbc_s, glu_s, dt_s = _in_proj_rows(
        x, g_mix, w_main_t, tp, ts)
    put_sample = lambda full, part: lax.dynamic_update_slice(full, part, (tp, 0))

    a_out = _attn_prompt(qb_p, ktb_p, vtb_p, prm['sb_bias'][i], bp, sp_len, tp + ts)
    seq3 = lambda a: a.reshape(bs, ss_len, MIX_WIDTH)
    a_sample = _attn_decode(seq3(qb_s), seq3(kb_s), seq3(vb_s), cache_k, cache_v, page_table,
                            prm['sb_bias'][i], i)
    a_out = put_sample(a_out, a_sample.reshape(ts, MIX_WIDTH))

    pad_lanes = lambda vec, fill: jnp.pad(vec, (0, LANES - vec.shape[0]),
                                          constant_values=fill).reshape(1, LANES)
    ssd_prm = (prm['ssm_conv_w'][i], row(prm['ssm_conv_b'][i]), pad_lanes(prm['dt_bias'][i], 0.0),
               pad_lanes(-jnp.exp(prm['a_log'][i]), 0.0),
               row(jnp.repeat(prm['d_skip'][i], SSM_HEAD_DIM)), row(prm['ssm_norm'][i]))
    zeros_conv = jnp.zeros((bp, SUBLANES, SSM_CONV_CH), F32)
    zeros_h = jnp.zeros((bp, SSM_HEADS, SSM_HEAD_DIM, SSM_STATE), F32)
    b_out, sconv_p, h_p = _ssd(xbc_p, dt_p, z_p, zeros_conv, zeros_h, *ssd_prm,
                               batch=bp, seq=sp_len, valid=sp_len, out_rows=tp + ts)
    padseq = lambda a: jnp.pad(a.reshape(bs, ss_len, -1),
                               ((0, 0), (0, SSM_CHUNK - ss_len), (0, 0))).reshape(bs * SSM_CHUNK, -1)
    b_s, sconv_s, h_s = _ssd(padseq(xbc_s), padseq(dt_s), padseq(z_s),
                             _pad_rows(state_ssm_conv[i], SUBLANES), state_ssm[i], *ssd_prm,
                             batch=bs, seq=SSM_CHUNK, valid=ss_len)
    b_sample = b_s.reshape(bs, SSM_CHUNK, MIX_WIDTH)[:, :ss_len].reshape(ts, MIX_WIDTH)
    b_out = put_sample(b_out, b_sample)

    cf_prm = (prm['cf_conv_w'][i], row(prm['cf_conv_b'][i]), row(prm['cf_ln_g'][i]),
              row(prm['cf_ln_b'][i]))
    c_out, cf_p = _conformer(glu_p, jnp.zeros((bp, CF_HALO, MIX_WIDTH), F32), *cf_prm,
                             batch=bp, seq=sp_len, out_rows=tp + ts)
    c_sample, cf_s = _conformer(glu_s, _pad_rows(state_cf_conv[i], CF_HALO), *cf_prm,
                                batch=bs, seq=ss_len)
    c_out = put_sample(c_out, c_sample)

    x = _merge(x, a_out, b_out, c_out, g_mix, w_gates_t,
               prm['w_branch'][i].astype(BF16), prm['w_out'][i].astype(BF16))

    w_router = jnp.concatenate(
        [prm['w_router_expert'][i].reshape(D_MODEL, N_EXPERTS), prm['w_router_group'][i],
         jnp.zeros((D_MODEL, LANES - N_EXPERTS - MOE_GROUPS), F32)], axis=1)
    b_router = jnp.concatenate(
        [prm['b_router_expert'][i].reshape(N_EXPERTS), prm['b_router_group'][i],
         jnp.zeros((LANES - N_EXPERTS - MOE_GROUPS,), F32)]).reshape(1, LANES)
    t_norm, comb = _router(x, row(prm['norm_ffn'][i]), w_router, b_router)
    x = _experts(x, t_norm, comb, prm['w_expert_gate'][i], prm['w_expert_up'][i],
                 prm['w_expert_down'][i])

    outs = _ple(x, p, row(prm['norm_ple'][i]), prm['w_ple_gate'][i].astype(BF16),
                prm['w_ple_proj'][i].astype(BF16), row(prm['norm_final']), final)

    conv_tail = SSM_CONV - 1
    cf_tail = CF_CONV - 1
    states = dict(
        k_p=jnp.transpose(kt_p.reshape(bp, SB_HEADS, SB_HEAD_DIM, sp_len), (0, 3, 1, 2)),
        v_p=jnp.transpose(vt_p.reshape(bp, SB_HEADS, SB_HEAD_DIM, sp_len), (0, 3, 1, 2)),
        h_p=h_p, sconv_p=sconv_p[:, SUBLANES - conv_tail:], cf_p=cf_p[:, CF_HALO - cf_tail:],
        k_s=k_s.reshape(bs, ss_len, SB_HEADS, SB_HEAD_DIM),
        v_s=v_s.reshape(bs, ss_len, SB_HEADS, SB_HEAD_DIM),
        h_s=h_s, sconv_s=sconv_s[:, SUBLANES - conv_tail:], cf_s=cf_s[:, CF_HALO - cf_tail:])
    return outs, states


def kernel(x_prompt, x_sample, p_prompt, p_sample, cache_k, cache_v, page_table, state_ssm, state_ssm_conv, state_cf_conv, norm_mix, w_in, sb_bias, ssm_conv_w, ssm_conv_b, dt_bias, a_log, d_skip, ssm_norm, cf_conv_w, cf_conv_b, cf_ln_g, cf_ln_b, w_branch, w_out, norm_ffn, w_router_group, b_router_group, w_router_expert, b_router_expert, w_expert_gate, w_expert_up, w_expert_down, norm_ple, w_ple_gate, w_ple_proj, norm_final):
    prm = dict(norm_mix=norm_mix, w_in=w_in, sb_bias=sb_bias, ssm_conv_w=ssm_conv_w,
               ssm_conv_b=ssm_conv_b, dt_bias=dt_bias, a_log=a_log, d_skip=d_skip,
               ssm_norm=ssm_norm, cf_conv_w=cf_conv_w, cf_conv_b=cf_conv_b, cf_ln_g=cf_ln_g,
               cf_ln_b=cf_ln_b, w_branch=w_branch, w_out=w_out, norm_ffn=norm_ffn,
               w_router_group=w_router_group, b_router_group=b_router_group,
               w_router_expert=w_router_expert, b_router_expert=b_router_expert,
               w_expert_gate=w_expert_gate, w_expert_up=w_expert_up,
               w_expert_down=w_expert_down, norm_ple=norm_ple, w_ple_gate=w_ple_gate,
               w_ple_proj=w_ple_proj, norm_final=norm_final)
    depth = w_in.shape[0]
    bp, sp_len, _ = x_prompt.shape
    bs, ss_len, _ = x_sample.shape
    tp = bp * sp_len
    ts = bs * ss_len
    dims = (bp, sp_len, bs, ss_len)
    n_pool = cache_k.shape[1]
    ck = jnp.transpose(cache_k, (0, 1, 3, 4, 2)).reshape(depth, n_pool, MIX_WIDTH, PAGE_SIZE)
    cv = jnp.transpose(cache_v, (0, 1, 3, 4, 2)).reshape(depth, n_pool, MIX_WIDTH, PAGE_SIZE)

    x = jnp.concatenate([x_prompt.reshape(tp, D_MODEL), x_sample.reshape(ts, D_MODEL)], axis=0)
    per_layer = []
    for i in range(depth):
        p = jnp.concatenate([p_prompt[i].reshape(tp, PLE_DIM), p_sample[i].reshape(ts, PLE_DIM)],
                            axis=0)
        outs, st = _layer(i, x, p, ck, cv, page_table, state_ssm, state_ssm_conv, state_cf_conv,
                          prm, dims, final=(i == depth - 1))
        x = outs[0]
        per_layer.append(st)
    y = outs[1]
    stack = lambda name: jnp.stack([st[name] for st in per_layer])
    return (y[:tp].reshape(bp, sp_len, D_MODEL), y[tp:].reshape(bs, ss_len, D_MODEL),
            stack('k_p'), stack('v_p'), stack('h_p'), stack('sconv_p'), stack('cf_p'),
            stack('k_s'), stack('v_s'), stack('h_s'), stack('sconv_s'), stack('cf_s'))
```

```python
import functools

import jax
import jax.numpy as jnp
from jax import lax
from jax.experimental import pallas as pl
from jax.experimental.pallas import tpu as pltpu

F32 = jnp.float32
BF16 = jnp.bfloat16
EPS = 1e-6

D_MODEL = 1024
MIX_WIDTH = 512
N_BRANCH = 3
SB_HEADS = 8
SB_HEAD_DIM = 64
SSM_HEADS = 8
SSM_HEAD_DIM = 64
SSM_GROUPS = 2
SSM_STATE = 128
SSM_CONV = 4
SSM_CONV_CH = MIX_WIDTH + 2 * SSM_GROUPS * SSM_STATE
SSM_CHUNK = 128
CF_CONV = 31
MOE_GROUPS = 4
EXPERTS_PER_GROUP = 4
N_EXPERTS = MOE_GROUPS * EXPERTS_PER_GROUP
D_EXPERT = 512
PLE_DIM = 256
PAGE_SIZE = 128
OFF_DT = 3 * MIX_WIDTH + MIX_WIDTH + SSM_CONV_CH
OFF_GLU = OFF_DT + SSM_HEADS
OFF_GATE = OFF_GLU + 2 * MIX_WIDTH

LANES = 128
SUBLANES = 8
VMEM_LIMIT = 56 * 1024 * 1024

TOKEN_TILE = 256
MOE_ROW_TILE = 256
ATTN_TILE = 256
CF_TILE = 256
DEC_PAGES_PER_STEP = 16


def _cparams(sem):
    return pltpu.CompilerParams(dimension_semantics=sem, vmem_limit_bytes=VMEM_LIMIT)


def _rmsnorm(x, g):
    return x * lax.rsqrt(jnp.mean(x * x, axis=-1, keepdims=True) + EPS) * g


def _sigmoid(x):
    return 1.0 / (1.0 + jnp.exp(-x))


def _silu(x):
    return x * _sigmoid(x)


def _softplus(x):
    return jnp.maximum(x, 0.0) + jnp.log(1.0 + jnp.exp(-jnp.abs(x)))


LOG2E = 1.4426950408889634


def _softplus2(x):
    return jnp.maximum(x, 0.0) + jnp.log2(1.0 + jnp.exp2(-jnp.abs(x)))


def _dot(a, b):
    return jnp.dot(a, b, preferred_element_type=F32)


def _dot_nt(a, b):
    return lax.dot_general(a, b, (((1,), (1,)), ((), ())), preferred_element_type=F32)


def _dot_tn(a, b):
    return lax.dot_general(a, b, (((0,), (0,)), ((), ())), preferred_element_type=F32)


def _split3(x):
    hi = x.astype(BF16)
    r = x - hi.astype(F32)
    mid = r.astype(BF16)
    lo = (r - mid.astype(F32)).astype(BF16)
    return hi, mid, lo


def _dot_exact_lhs(mat_bf16, x):
    hi, mid, lo = _split3(x)
    return _dot(mat_bf16, hi) + _dot(mat_bf16, mid) + _dot(mat_bf16, lo)


IN_PROJ_WIDTHS = [MIX_WIDTH, MIX_WIDTH, MIX_WIDTH, MIX_WIDTH, MIX_WIDTH, MIX_WIDTH,
                  SSM_CONV_CH, 2 * MIX_WIDTH, LANES]
IN_PROJ_DTYPES = [BF16, F32, F32, BF16, BF16, F32, F32, F32, F32]


def _in_proj_kernel(x_ref, g_ref, w_ref, q_ref, k_ref, v_ref, kb_ref, vb_ref,
                    z_ref, xbc_ref, glu_ref, dt_ref, *, transposed_kv):
    h = _rmsnorm(x_ref[...], g_ref[...]).astype(BF16)
    w = MIX_WIDTH
    if transposed_kv:
        q = _dot_nt(h, w_ref[0:w, :])
        kv = _dot_nt(w_ref[w:3 * w, :], h)
        k = kv[0:w]
        v = kv[w:2 * w]
        rest = _dot_nt(h, w_ref[3 * w:, :])
    else:
        p = _dot_nt(h, w_ref[...])
        q = p[:, 0:w]
        k = p[:, w:2 * w]
        v = p[:, 2 * w:3 * w]
        rest = p[:, 3 * w:]
    q_ref[...] = (q * (SB_HEAD_DIM ** -0.5 * LOG2E)).astype(BF16)
    k_ref[...] = k
    v_ref[...] = v
    kb_ref[...] = k.astype(BF16)
    vb_ref[...] = v.astype(BF16)
    z_ref[...] = rest[:, 0:w]
    xbc_ref[...] = rest[:, w:3 * w]
    glu_ref[...] = rest[:, 3 * w:5 * w]
    dt_ref[...] = rest[:, 5 * w:5 * w + LANES]


def _in_proj_prompt(x, g, w_main_t, batch, seq):
    tm = TOKEN_TILE
    nt = seq // tm
    n = w_main_t.shape[0]
    row = lambda width: pl.BlockSpec((tm, width), lambda b, j: (b * nt + j, 0))
    par = lambda shape: pl.BlockSpec(shape, lambda b, j: (0,) * len(shape))
    out_specs = [row(wd) for wd in IN_PROJ_WIDTHS]
    out_shape = [jax.ShapeDtypeStruct((batch * seq, wd), dt)
                 for wd, dt in zip(IN_PROJ_WIDTHS, IN_PROJ_DTYPES)]
    for idx in (1, 2):
        out_specs[idx] = pl.BlockSpec((None, MIX_WIDTH, tm), lambda b, j: (b, 0, j))
        out_shape[idx] = jax.ShapeDtypeStruct((batch, MIX_WIDTH, seq), F32)
    for idx in (3, 4):
        out_specs[idx] = pl.BlockSpec((None, None, MIX_WIDTH, tm), lambda b, j: (b, j, 0, 0))
        out_shape[idx] = jax.ShapeDtypeStruct((batch, nt, MIX_WIDTH, tm), BF16)
    return pl.pallas_call(
        functools.partial(_in_proj_kernel, transposed_kv=True),
        grid=(batch, nt),
        in_specs=[row(D_MODEL), par((1, D_MODEL)), par((n, D_MODEL))],
        out_specs=out_specs,
        out_shape=out_shape,
        compiler_params=_cparams(("parallel", "parallel")),
        name="in_proj_prompt",
    )(x, g, w_main_t)


def _in_proj_rows(x, g, w_main_t, row_start, rows):
    tm = TOKEN_TILE
    first = row_start // tm
    n = w_main_t.shape[0]
    par = lambda shape: pl.BlockSpec(shape, lambda i: (0,) * len(shape))
    return pl.pallas_call(
        functools.partial(_in_proj_kernel, transposed_kv=False),
        grid=(rows // tm,),
        in_specs=[pl.BlockSpec((tm, D_MODEL), lambda i: (i + first, 0)),
                  par((1, D_MODEL)), par((n, D_MODEL))],
        out_specs=[pl.BlockSpec((tm, wd), lambda i: (i, 0)) for wd in IN_PROJ_WIDTHS],
        out_shape=[jax.ShapeDtypeStruct((rows, wd), dt)
                   for wd, dt in zip(IN_PROJ_WIDTHS, IN_PROJ_DTYPES)],
        compiler_params=_cparams(("parallel",)),
        name="in_proj_rows",
    )(x, g, w_main_t)


def _strict_upper(n):
    r = lax.broadcasted_iota(jnp.int32, (n, n), 0)
    c = lax.broadcasted_iota(jnp.int32, (n, n), 1)
    return jnp.where(r > c, 1.0, 0.0).astype(BF16)


def _sb_block(z, v_blk, upper, carry, mask, v_channel_major=False):
    sp = _softplus2(z)
    if mask is not None:
        sp = jnp.where(mask, sp, 0.0)
    after = _dot(sp.astype(BF16), upper)
    w = jnp.exp2(z - sp - after - carry)
    if mask is not None:
        w = jnp.where(mask, w, 0.0)
    pv = _dot_nt if v_channel_major else _dot
    return pv(w.astype(BF16), v_blk), jnp.sum(sp, axis=1, keepdims=True)


def _attn_prompt_kernel(bias_ref, q_ref, k_ref, v_ref, o_ref, acc_ref, carry_ref, *, seq, tile):
    hp = pl.program_id(1)
    n_q = seq // tile
    upper = _strict_upper(tile)
    lane = lax.broadcasted_iota(jnp.int32, (tile, LANES), 1)
    head0 = lane < SB_HEAD_DIM
    r = lax.broadcasted_iota(jnp.int32, (tile, tile), 0)
    c = lax.broadcasted_iota(jnp.int32, (tile, tile), 1)
    diag_mask = jnp.concatenate([c < r, c < r], axis=0)
    bias0 = bias_ref[hp * 2] * LOG2E
    bias1 = bias_ref[hp * 2 + 1] * LOG2E

    def visit(qis, kj, mask):
        zs, sps, afters, carries = [], [], [], []
        for qi in qis:
            q = q_ref[pl.ds(pl.multiple_of(qi * tile, tile), tile), :]
            zero = jnp.zeros_like(q)
            q2 = jnp.concatenate([jnp.where(head0, q, zero), jnp.where(head0, zero, q)], axis=0)
            d = _dot(q2, k_ref[kj])
            zs.append(jnp.concatenate([d[:tile] + bias0, d[tile:] + bias1], axis=0))
        for z in zs:
            sp = _softplus2(z)
            sps.append(sp if mask is None else jnp.where(mask, sp, 0.0))
        for sp in sps:
            afters.append(_dot(sp.astype(BF16), upper))
        contribs = []
        for qi, z, sp, after in zip(qis, zs, sps, afters):
            if mask is None:
                carry = carry_ref[qi]
                carries.append(carry)
                after = after + jnp.concatenate([carry] * (tile // LANES), axis=1)
            w = jnp.exp2(z - sp - after)
            if mask is not None:
                w = jnp.where(mask, w, 0.0)
            contribs.append(_dot_nt(w.astype(BF16), v_ref[kj]))
        for n, (qi, sp, contrib) in enumerate(zip(qis, sps, contribs)):
            row_sum = jnp.broadcast_to(jnp.sum(sp, axis=1, keepdims=True), (2 * tile, LANES))
            if mask is None:
                acc_ref[qi] += contrib
                carry_ref[qi] = carries[n] + row_sum
            else:
                acc_ref[qi] = contrib
                carry_ref[qi] = row_sum

    def key_tile(jj, _):
        kj = n_q - 1 - jj
        visit([kj], kj, diag_mask)

        def q_pair(i, _):
            visit([kj + 1 + 2 * i, kj + 2 + 2 * i], kj, None)
            return 0

        lax.fori_loop(0, jj // 2, q_pair, 0)

        @pl.when(jj % 2 == 1)
        def _():
            visit([n_q - 1], kj, None)

        return 0

    lax.fori_loop(0, n_q, key_tile, 0)

    def write(qi, _):
        acc = acc_ref[qi]
        o_ref[pl.ds(pl.multiple_of(qi * tile, tile), tile), :] = jnp.where(
            head0, acc[:tile], acc[tile:])
        return 0

    lax.fori_loop(0, n_q, write, 0)


def _attn_prompt(q, kt, vt, sb_bias, batch, seq):
    tile = kt.shape[-1]
    nt = seq // tile
    blk = pl.BlockSpec((seq, LANES), lambda b, hp: (b, hp))
    kv_blk = pl.BlockSpec((None, nt, LANES, tile), lambda b, hp: (b, 0, hp, 0))
    return pl.pallas_call(
        functools.partial(_attn_prompt_kernel, seq=seq, tile=tile),
        grid=(batch, SB_HEADS // 2),
        in_specs=[pl.BlockSpec(memory_space=pltpu.SMEM), blk, kv_blk, kv_blk],
        out_specs=blk,
        out_shape=jax.ShapeDtypeStruct((batch * seq, MIX_WIDTH), F32),
        scratch_shapes=[pltpu.VMEM((nt, 2 * tile, LANES), F32),
                        pltpu.VMEM((nt, 2 * tile, LANES), F32)],
        compiler_params=_cparams(("parallel", "parallel")),
        name="attn_prompt",
    )(sb_bias, q, kt, vt)


def _attn_decode_kernel(pt_ref, bias_ref, q_ref, kn_ref, vn_ref, *rest, pages, dec_seq):
    k_refs = rest[:pages]
    v_refs = rest[pages:2 * pages]
    o_ref = rest[2 * pages]
    acc_ref, carry_ref = rest[2 * pages + 1:]
    step = pl.program_id(1)
    rows = SB_HEADS * dec_seq
    width = MIX_WIDTH
    upper = _strict_upper(PAGE_SIZE)
    r_w = lax.broadcasted_iota(jnp.int32, (rows, width), 0)
    c_w = lax.broadcasted_iota(jnp.int32, (rows, width), 1)
    head_mask = (r_w // dec_seq) == (c_w // SB_HEAD_DIM)
    q_rep = jnp.concatenate([q_ref[0].astype(F32)] * SB_HEADS, axis=0)
    q_bd = jnp.where(head_mask, q_rep, 0.0).astype(BF16)
    r_k = lax.broadcasted_iota(jnp.int32, (rows, PAGE_SIZE), 0)
    c_k = lax.broadcasted_iota(jnp.int32, (rows, PAGE_SIZE), 1)
    bias = jnp.zeros((rows, PAGE_SIZE), F32)
    for h in range(SB_HEADS):
        bias = jnp.where(r_k // dec_seq == h, bias_ref[h] * LOG2E, bias)

    @pl.when(step == 0)
    def _():
        pad = jnp.zeros((PAGE_SIZE - dec_seq, width), F32)
        kn = jnp.concatenate([kn_ref[0].astype(F32), pad], axis=0).astype(BF16)
        vn = jnp.concatenate([vn_ref[0].astype(F32), pad], axis=0).astype(BF16)
        z = _dot_nt(q_bd, kn) + bias
        mask = c_k < (r_k % dec_seq)
        contrib, rs = _sb_block(z, vn, upper, 0.0, mask)
        acc_ref[...] = contrib
        carry_ref[...] = jnp.broadcast_to(rs, (rows, PAGE_SIZE))

    k_all = jnp.concatenate([k_refs[j][...].astype(BF16) for j in range(pages)], axis=1)
    z = _dot(q_bd, k_all)
    z = jnp.concatenate([z[:, j * PAGE_SIZE:(j + 1) * PAGE_SIZE] + bias for j in range(pages)],
                        axis=0)
    sp = _softplus2(z)
    after = _dot(sp.astype(BF16), upper)
    row_sum = jnp.sum(sp, axis=1, keepdims=True)
    carry = carry_ref[...]
    carries = []
    for j in range(pages):
        carries.append(carry)
        carry = carry + row_sum[j * rows:(j + 1) * rows]
    w = jnp.exp2(z - sp - after - jnp.concatenate(carries, axis=0)).astype(BF16)
    w_all = jnp.concatenate([w[j * rows:(j + 1) * rows] for j in range(pages)], axis=1)
    v_all = jnp.concatenate([v_refs[j][...].astype(BF16) for j in range(pages)], axis=1)
    acc = acc_ref[...] + _dot_nt(w_all, v_all)
    acc_ref[...] = acc
    carry_ref[...] = carry

    @pl.when(step == pl.num_programs(1) - 1)
    def _():
        a = jnp.where(head_mask, acc, 0.0)
        out = a[0:dec_seq]
        for h in range(1, SB_HEADS):
            out = out + a[h * dec_seq:(h + 1) * dec_seq]
        o_ref[0] = out


def _attn_decode(q, k_new, v_new, cache_k, cache_v, page_table, sb_bias, layer):
    bsz, dec_seq, width = q.shape
    n_pages = page_table.shape[1]
    pages = DEC_PAGES_PER_STEP
    assert n_pages % pages == 0 and dec_seq == SUBLANES
    steps = n_pages // pages
    rows = SB_HEADS * dec_seq

    def page_spec(j):
        def imap(b, s, pt):
            return (layer, pt[b * n_pages + (n_pages - 1 - (s * pages + j))], 0, 0)
        return pl.BlockSpec((None, None, width, PAGE_SIZE), imap)

    seq_spec = pl.BlockSpec((1, dec_seq, width), lambda b, s, pt: (b, 0, 0))
    grid_spec = pltpu.PrefetchScalarGridSpec(
        num_scalar_prefetch=1,
        grid=(bsz, steps),
        in_specs=[pl.BlockSpec(memory_space=pltpu.SMEM), seq_spec, seq_spec, seq_spec]
        + [page_spec(j) for j in range(pages)] * 2,
        out_specs=seq_spec,
        scratch_shapes=[pltpu.VMEM((rows, width), F32), pltpu.VMEM((rows, PAGE_SIZE), F32)],
    )
    return pl.pallas_call(
        functools.partial(_attn_decode_kernel, pages=pages, dec_seq=dec_seq),
        grid_spec=grid_spec,
        out_shape=jax.ShapeDtypeStruct((bsz, dec_seq, width), F32),
        compiler_params=_cparams(("parallel", "arbitrary")),
        name="attn_decode",
    )(page_table.reshape(-1), sb_bias, q, k_new, v_new,
      *([cache_k] * pages), *([cache_v] * pages))


def _expand_heads(cols, width):
    m = cols.shape[0]
    lane = lax.broadcasted_iota(jnp.int32, (m, width), 1)
    out = jnp.zeros((m, width), F32)
    for h in range(width // SSM_HEAD_DIM):
        out = jnp.where(lane // SSM_HEAD_DIM == h, cols[:, h:h + 1], out)
    return out


def _ssd_kernel(xbc_ref, dt_ref, z_ref, buf_ref, h0_ref, cw_ref, cb_ref, dtb_ref, a_ref,
                dskip_ref, ng_ref, y_ref, conv_ref, h_ref, xp_ref, hs_ref, *, chunk, valid):
    c = pl.program_id(1)
    q = chunk
    halo = SUBLANES

    @pl.when(c == 0)
    def _():
        xp_ref[0:halo, :] = buf_ref[0]
        hs_ref[...] = h0_ref[0]

    xp_ref[halo:halo + q, :] = xbc_ref[...]
    acc = cb_ref[...] + cw_ref[SSM_CONV - 1:SSM_CONV, :] * xp_ref[halo:halo + q, :]
    for j in range(SSM_CONV - 1):
        off = halo - (SSM_CONV - 1) + j
        acc = acc + cw_ref[j:j + 1, :] * xp_ref[off:off + q, :]
    last = min(valid, q)
    tail = xp_ref[last:last + halo, :]
    conv_ref[0] = tail
    xp_ref[0:halo, :] = tail
    xc = _silu(acc)
    xs = xc[:, 0:MIX_WIDTH]
    gw = SSM_GROUPS * SSM_STATE
    bm = xc[:, MIX_WIDTH:MIX_WIDTH + gw]
    cm = xc[:, MIX_WIDTH + gw:MIX_WIDTH + 2 * gw]

    dt = _softplus(dt_ref[...] + dtb_ref[...])
    if valid < q:
        row = lax.broadcasted_iota(jnp.int32, (q, LANES), 0)
        dt = jnp.where(row < valid, dt, 0.0)
    da = dt * a_ref[...]
    r = lax.broadcasted_iota(jnp.int32, (q, q), 0)
    s = lax.broadcasted_iota(jnp.int32, (q, q), 1)
    causal = s <= r
    tril = jnp.where(causal, 1.0, 0.0).astype(BF16)
    cum = _dot_exact_lhs(tril, da)
    cum_t = jnp.transpose(cum)
    dt_t = jnp.transpose(dt)
    cum_last = cum[q - 1:q, :]
    w_end = jnp.exp(cum_last - cum) * dt
    e_cum = jnp.exp(cum)

    hpg = SSM_HEADS // SSM_GROUPS
    gwid = hpg * SSM_HEAD_DIM
    lane_g = lax.broadcasted_iota(jnp.int32, (q, gwid), 1)
    xs_b = xs.astype(BF16)
    e_cum_x = _expand_heads(e_cum, MIX_WIDTH)
    w_end_x = _expand_heads(w_end, MIX_WIDTH)
    xw = (xs * w_end_x).astype(BF16)
    y_parts = []
    for g in range(SSM_GROUPS):
        b_g = bm[:, g * SSM_STATE:(g + 1) * SSM_STATE].astype(BF16)
        c_g = cm[:, g * SSM_STATE:(g + 1) * SSM_STATE].astype(BF16)
        cb = _dot_nt(c_g, b_g)
        xs_g = xs_b[:, g * gwid:(g + 1) * gwid]
        y_g = jnp.zeros((q, gwid), F32)
        for e in range(hpg):
            hd = g * hpg + e
            seg = cum[:, hd:hd + 1] - cum_t[hd:hd + 1, :]
            decay = jnp.where(causal, jnp.exp(jnp.where(causal, seg, 0.0)), 0.0)
            w = (cb * decay * dt_t[hd:hd + 1, :]).astype(BF16)
            y_g = jnp.where(lane_g // SSM_HEAD_DIM == e, _dot(w, xs_g), y_g)
        h_g = hs_ref[g * hpg:(g + 1) * hpg].reshape(gwid, SSM_STATE)
        y_prev = _dot_nt(c_g, h_g.astype(BF16))
        y_parts.append(y_g + y_prev * e_cum_x[:, g * gwid:(g + 1) * gwid])
        upd = _dot_tn(xw[:, g * gwid:(g + 1) * gwid], b_g)
        for e in range(hpg):
            hd = g * hpg + e
            scale = jnp.exp(cum_last[:, hd:hd + 1])
            hs_ref[hd] = hs_ref[hd] * scale + upd[e * SSM_HEAD_DIM:(e + 1) * SSM_HEAD_DIM, :]
    y = jnp.concatenate(y_parts, axis=1) + dskip_ref[...] * xs
    y = y * _silu(z_ref[...])
    y_ref[...] = _rmsnorm(y, ng_ref[...])
    h_ref[0] = hs_ref[...]


def _ssd(xbc, dt_raw, z, conv_buf, h0, conv_w, conv_b, dt_bias, a_neg, d_skip_row, norm_g,
         batch, seq, valid):
    q = SSM_CHUNK
    nc = seq // q
    row = lambda width: pl.BlockSpec((q, width), lambda b, c: (b * nc + c, 0))
    par = lambda shape: pl.BlockSpec(shape, lambda b, c: (0,) * len(shape))
    return pl.pallas_call(
        functools.partial(_ssd_kernel, chunk=q, valid=valid),
        grid=(batch, nc),
        in_specs=[row(SSM_CONV_CH), row(LANES), row(MIX_WIDTH),
                  pl.BlockSpec((1, SUBLANES, SSM_CONV_CH), lambda b, c: (b, 0, 0)),
                  pl.BlockSpec((1, SSM_HEADS, SSM_HEAD_DIM, SSM_STATE), lambda b, c: (b, 0, 0, 0)),
                  par((SSM_CONV, SSM_CONV_CH)), par((1, SSM_CONV_CH)), par((1, LANES)),
                  par((1, LANES)), par((1, MIX_WIDTH)), par((1, MIX_WIDTH))],
        out_specs=[row(MIX_WIDTH),
                   pl.BlockSpec((1, SUBLANES, SSM_CONV_CH), lambda b, c: (b, 0, 0)),
                   pl.BlockSpec((1, SSM_HEADS, SSM_HEAD_DIM, SSM_STATE), lambda b, c: (b, 0, 0, 0))],
        out_shape=[jax.ShapeDtypeStruct((batch * seq, MIX_WIDTH), F32),
                   jax.ShapeDtypeStruct((batch, SUBLANES, SSM_CONV_CH), F32),
                   jax.ShapeDtypeStruct((batch, SSM_HEADS, SSM_HEAD_DIM, SSM_STATE), F32)],
        scratch_shapes=[pltpu.VMEM((q + 2 * SUBLANES, SSM_CONV_CH), F32),
                        pltpu.VMEM((SSM_HEADS, SSM_HEAD_DIM, SSM_STATE), F32)],
        compiler_params=_cparams(("parallel", "arbitrary")),
        name="ssd",
    )(xbc, dt_raw, z, conv_buf, h0, conv_w, conv_b, dt_bias, a_neg, d_skip_row, norm_g)


CF_HALO = 32


def _conformer_kernel(glu_ref, buf_ref, cw_ref, cb_ref, lg_ref, lb_ref, y_ref, new_ref,
                      xp_ref, *, tile):
    c = pl.program_id(1)

    @pl.when(c == 0)
    def _():
        xp_ref[0:CF_HALO, :] = buf_ref[0]

    glu = glu_ref[...]
    u = glu[:, 0:MIX_WIDTH] * _sigmoid(glu[:, MIX_WIDTH:2 * MIX_WIDTH])
    xp_ref[CF_HALO:CF_HALO + tile, :] = u
    acc = cb_ref[...] + cw_ref[CF_CONV - 1:CF_CONV, :] * u
    for j in range(CF_CONV - 1):
        off = CF_HALO - (CF_CONV - 1) + j
        acc = acc + cw_ref[j:j + 1, :] * xp_ref[off:off + tile, :]
    tail = xp_ref[tile:tile + CF_HALO, :]
    new_ref[0] = tail
    xp_ref[0:CF_HALO, :] = tail
    mu = jnp.mean(acc, axis=-1, keepdims=True)
    xc = acc - mu
    var = jnp.mean(xc * xc, axis=-1, keepdims=True)
    y_ref[...] = _silu(xc * lax.rsqrt(var + EPS) * lg_ref[...] + lb_ref[...])


def _conformer(glu, conv_buf, conv_w, conv_b, ln_g, ln_b, batch, seq):
    tile = CF_TILE if seq % CF_TILE == 0 else seq
    nt = seq // tile
    par = lambda shape: pl.BlockSpec(shape, lambda b, c: (0,) * len(shape))
    buf_spec = pl.BlockSpec((1, CF_HALO, MIX_WIDTH), lambda b, c: (b, 0, 0))
    return pl.pallas_call(
        functools.partial(_conformer_kernel, tile=tile),
        grid=(batch, nt),
        in_specs=[pl.BlockSpec((tile, 2 * MIX_WIDTH), lambda b, c: (b * nt + c, 0)), buf_spec,
                  par((CF_CONV, MIX_WIDTH)), par((1, MIX_WIDTH)), par((1, MIX_WIDTH)),
                  par((1, MIX_WIDTH))],
        out_specs=[pl.BlockSpec((tile, MIX_WIDTH), lambda b, c: (b * nt + c, 0)), buf_spec],
        out_shape=[jax.ShapeDtypeStruct((batch * seq, MIX_WIDTH), F32),
                   jax.ShapeDtypeStruct((batch, CF_HALO, MIX_WIDTH), F32)],
        scratch_shapes=[pltpu.VMEM((tile + CF_HALO, MIX_WIDTH), F32)],
        compiler_params=_cparams(("parallel", "arbitrary")),
        name="conformer",
    )(glu, conv_buf, conv_w, conv_b, ln_g, ln_b)


def _merge_kernel(x_ref, ap_ref, bp_ref, cp_ref, as_ref, bs_ref, cs_ref, g_ref, wg_ref, wb_ref,
                  wo_ref, o_ref, *, prompt_tiles):
    is_prompt = pl.program_id(0) < prompt_tiles
    x = x_ref[...]
    h = _rmsnorm(x, g_ref[...]).astype(BF16)
    merged = jnp.zeros(x.shape, F32)
    for n, (br_p, br_s) in enumerate(((ap_ref, as_ref), (bp_ref, bs_ref), (cp_ref, cs_ref))):
        br = jnp.where(is_prompt, br_p[...], br_s[...])
        gate = _sigmoid(_dot_nt(h, wg_ref[n * D_MODEL:(n + 1) * D_MODEL, :]))
        merged = merged + gate * _dot(br.astype(BF16), wb_ref[n])
    o_ref[...] = x + _dot(merged.astype(BF16), wo_ref[...])


def _merge(x, prompt_branches, sample_branches, g, w_gates, w_branch, w_out):
    t = x.shape[0]
    tm = TOKEN_TILE
    prompt_tiles = prompt_branches[0].shape[0] // tm
    assert sample_branches[0].shape[0] == tm and prompt_tiles + 1 == t // tm
    row = lambda width: pl.BlockSpec((tm, width), lambda i: (i, 0))
    p_row = pl.BlockSpec((tm, MIX_WIDTH), lambda i: (jnp.minimum(i, prompt_tiles - 1), 0))
    s_row = pl.BlockSpec((tm, MIX_WIDTH), lambda i: (0, 0))
    par = lambda shape: pl.BlockSpec(shape, lambda i: (0,) * len(shape))
    return pl.pallas_call(
        functools.partial(_merge_kernel, prompt_tiles=prompt_tiles),
        grid=(t // tm,),
        in_specs=[row(D_MODEL), p_row, p_row, p_row, s_row, s_row, s_row,
                  par((1, D_MODEL)), par((N_BRANCH * D_MODEL, D_MODEL)),
                  par((N_BRANCH, MIX_WIDTH, D_MODEL)), par((D_MODEL, D_MODEL))],
        out_specs=row(D_MODEL),
        out_shape=jax.ShapeDtypeStruct((t, D_MODEL), F32),
        compiler_params=_cparams(("parallel",)),
        name="merge",
    )(x, *prompt_branches, *sample_branches, g, w_gates, w_branch, w_out)


META_GROUP_LANE = N_EXPERTS
META_RANK_LANE = N_EXPERTS + 1


def _router_kernel(x_ref, g_ref, wr_ref, br_ref, tx_ref, cnt_ref, run_ref):
    i = pl.program_id(0)

    @pl.when(i == 0)
    def _():
        run_ref[...] = jnp.zeros_like(run_ref)

    t = _rmsnorm(x_ref[...], g_ref[...])
    tx_ref[:, 0:D_MODEL] = t
    logits = jnp.dot(t, wr_ref[...], preferred_element_type=F32,
                     precision=lax.Precision.HIGHEST) + br_ref[...]
    lane = lax.broadcasted_iota(jnp.int32, logits.shape, 1)
    neg = -jnp.inf
    is_group = (lane >= N_EXPERTS) & (lane < N_EXPERTS + MOE_GROUPS)
    gl = jnp.where(is_group, logits, neg)
    gmax = jnp.max(gl, axis=1, keepdims=True)
    g_idx = jnp.min(jnp.where(gl == gmax, lane, LANES), axis=1, keepdims=True) - N_EXPERTS
    g_w = 1.0 / jnp.sum(jnp.where(is_group, jnp.exp(logits - gmax), 0.0), axis=1, keepdims=True)
    in_group = (lane // EXPERTS_PER_GROUP) == g_idx
    el = jnp.where(in_group, logits, neg)
    v1 = jnp.max(el, axis=1, keepdims=True)
    i1 = jnp.min(jnp.where(el == v1, lane, LANES), axis=1, keepdims=True)
    el2 = jnp.where(lane == i1, neg, el)
    v2 = jnp.max(el2, axis=1, keepdims=True)
    i2 = jnp.min(jnp.where(el2 == v2, lane, LANES), axis=1, keepdims=True)
    e21 = jnp.exp(v2 - v1)
    p1 = 1.0 / (1.0 + e21)
    p2 = e21 * p1
    comb = jnp.where(lane == i1, p1 * g_w, 0.0) + jnp.where(lane == i2, p2 * g_w, 0.0)
    tm = logits.shape[0]
    onehot = jnp.where(lane == g_idx, 1.0, 0.0)
    r = lax.broadcasted_iota(jnp.int32, (tm, tm), 0)
    c = lax.broadcasted_iota(jnp.int32, (tm, tm), 1)
    tril = jnp.where(c <= r, 1.0, 0.0).astype(BF16)
    incl = _dot(tril, onehot.astype(BF16)) + run_ref[...]
    rank = jnp.sum(onehot * incl, axis=1, keepdims=True) - 1.0
    run_ref[...] = incl[tm - 1:tm, :]
    cnt_ref[...] = incl[tm - 1:tm, :]
    meta = jnp.where(lane == META_GROUP_LANE, g_idx.astype(F32), comb)
    tx_ref[:, D_MODEL:] = jnp.where(lane == META_RANK_LANE, rank, meta)


def _router(x, g, w_router, b_router):
    t = x.shape[0]
    tm = TOKEN_TILE
    row = lambda width: pl.BlockSpec((tm, width), lambda i: (i, 0))
    par = lambda shape: pl.BlockSpec(shape, lambda i: (0,) * len(shape))
    return pl.pallas_call(
        _router_kernel,
        grid=(t // tm,),
        in_specs=[row(D_MODEL), par((1, D_MODEL)), par((D_MODEL, LANES)), par((1, LANES))],
        out_specs=[row(D_MODEL + LANES), par((1, LANES))],
        out_shape=[jax.ShapeDtypeStruct((t, D_MODEL + LANES), F32),
                   jax.ShapeDtypeStruct((1, LANES), F32)],
        scratch_shapes=[pltpu.VMEM((1, LANES), F32)],
        compiler_params=_cparams(("arbitrary",)),
        name="router",
    )(x, g, w_router, b_router)


def _experts_kernel(tile_group_ref, n_tiles_ref, src_ref, src_next_ref, dst_ref, tx_hbm,
                    wg_ref, wu_ref, wd_ref, y_hbm, xs_ref, ys_ref, gsem, ssem, *, tm):
    i = pl.program_id(0)
    n_valid = n_tiles_ref[0]
    slot = i % 2

    def gather_copy(idx_ref, r, s):
        return pltpu.make_async_copy(tx_hbm.at[pl.ds(idx_ref[0, 0, r], 1)],
                                     xs_ref.at[s, pl.ds(r, 1)], gsem.at[s])

    def scatter_copy(r, s):
        return pltpu.make_async_copy(ys_ref.at[s, pl.ds(r, 1)],
                                     y_hbm.at[pl.ds(dst_ref[0, 0, r], 1)], ssem.at[s])

    def start_gather(idx_ref, s):
        def body(r, _):
            gather_copy(idx_ref, r, s).start()
            return 0
        lax.fori_loop(0, tm, body, 0, unroll=8)

    def wait_gather(s):
        pltpu.make_async_copy(tx_hbm.at[pl.ds(0, tm)], xs_ref.at[s], gsem.at[s]).wait()

    def wait_scatter(s):
        pltpu.make_async_copy(ys_ref.at[s], y_hbm.at[pl.ds(0, tm)], ssem.at[s]).wait()

    @pl.when(i == 0)
    def _():
        n_tok = tx_hbm.shape[0]
        ys_ref[0] = jnp.zeros((tm, D_MODEL), F32)
        fills = [pltpu.make_async_copy(ys_ref.at[0], y_hbm.at[pl.ds(n_tok + k * tm, tm)],
                                       ssem.at[0])
                 for k in range((y_hbm.shape[0] - n_tok) // tm)]
        for f in fills:
            f.start()
        for f in fills:
            f.wait()

    @pl.when((i == 0) & (n_valid > 0))
    def _():
        start_gather(src_ref, 0)

    @pl.when(i + 1 < n_valid)
    def _():
        start_gather(src_next_ref, 1 - slot)

    @pl.when(i < n_valid)
    def _():
        wait_gather(slot)

        @pl.when(i >= 2)
        def _():
            wait_scatter(slot)

        rows = xs_ref[slot]
        x = rows[:, 0:D_MODEL].astype(BF16)
        meta = rows[:, D_MODEL:]
        lane = lax.broadcasted_iota(jnp.int32, meta.shape, 1)
        first = tile_group_ref[i] * EXPERTS_PER_GROUP
        acc = jnp.zeros((tm, D_MODEL), F32)
        for e in range(EXPERTS_PER_GROUP):
            w_e = jnp.sum(jnp.where(lane == first + e, meta, 0.0), axis=1, keepdims=True)
            hg = _dot(x, wg_ref[e])
            hu = _dot(x, wu_ref[e])
            act = (_silu(hg) * hu * w_e).astype(BF16)
            acc = acc + _dot(act, wd_ref[e])
        ys_ref[slot] = acc

        def body(r, _):
            scatter_copy(r, slot).start()
            return 0
        lax.fori_loop(0, tm, body, 0, unroll=8)

    @pl.when(i == pl.num_programs(0) - 1)
    def _():
        @pl.when(n_valid >= 1)
        def _():
            wait_scatter((n_valid - 1) % 2)

        @pl.when(n_valid >= 2)
        def _():
            wait_scatter(n_valid % 2)


def _moe_plan(tx, counts, tm):
    t = tx.shape[0]
    group = tx[:, D_MODEL + META_GROUP_LANE].astype(jnp.int32)
    rank = tx[:, D_MODEL + META_RANK_LANE].astype(jnp.int32)
    count = counts[0, :MOE_GROUPS].astype(jnp.int32)
    padded = (count + tm - 1) // tm * tm
    ends = jnp.cumsum(padded)
    starts = ends - padded
    tiles = t // tm + MOE_GROUPS
    rows = tiles * tm
    dest = starts[group] + rank
    token = jnp.arange(t, dtype=jnp.int32)
    src = jnp.zeros((rows,), jnp.int32).at[dest].set(token)
    is_real = jnp.zeros((rows,), jnp.bool_).at[dest].set(True)
    spare = t - 1 + jnp.cumsum((~is_real).astype(jnp.int32))
    dst = jnp.where(is_real, src, spare)
    n_tiles = ends[MOE_GROUPS - 1] // tm
    tile_start = jnp.arange(tiles, dtype=jnp.int32) * tm
    tile_group = jnp.sum((tile_start[:, None] >= ends[None, :]).astype(jnp.int32), axis=1)
    last_group = tile_group[jnp.maximum(n_tiles - 1, 0)]
    tile_group = jnp.where(jnp.arange(tiles) < n_tiles, tile_group, last_group)
    tile_group = jnp.minimum(tile_group, MOE_GROUPS - 1)
    return (tile_group, n_tiles.reshape(1), src.reshape(tiles, 1, tm), dst.reshape(tiles, 1, tm))


def _experts(tx, tile_group, n_tiles, src, dst, w_gate, w_up, w_down, out_rows):
    tiles, _, tm = src.shape
    idx_spec = lambda shift: pl.BlockSpec(
        (1, 1, tm), lambda i, tg, nt: (jnp.minimum(i + shift, tiles - 1), 0, 0),
        memory_space=pltpu.SMEM)
    w_spec = lambda a, b: pl.BlockSpec((None, EXPERTS_PER_GROUP, a, b),
                                       lambda i, tg, nt: (tg[i], 0, 0, 0))
    grid_spec = pltpu.PrefetchScalarGridSpec(
        num_scalar_prefetch=2,
        grid=(tiles,),
        in_specs=[idx_spec(0), idx_spec(1), idx_spec(0),
                  pl.BlockSpec(memory_space=pl.ANY),
                  w_spec(D_MODEL, D_EXPERT), w_spec(D_MODEL, D_EXPERT), w_spec(D_EXPERT, D_MODEL)],
        out_specs=pl.BlockSpec(memory_space=pl.ANY),
        scratch_shapes=[pltpu.VMEM((2, tm, D_MODEL + LANES), F32),
                        pltpu.VMEM((2, tm, D_MODEL), F32),
                        pltpu.SemaphoreType.DMA((2,)), pltpu.SemaphoreType.DMA((2,))],
    )
    return pl.pallas_call(
        functools.partial(_experts_kernel, tm=tm),
        grid_spec=grid_spec,
        out_shape=jax.ShapeDtypeStruct((out_rows, D_MODEL), F32),
        compiler_params=_cparams(("arbitrary",)),
        name="experts",
    )(tile_group, n_tiles, src, src, dst, tx, w_gate, w_up, w_down)


def _ple_kernel(x_ref, moe_ref, p_ref, g_ref, wg_ref, wp_ref, gf_ref, o_ref, *maybe_y_ref):
    x = x_ref[...] + moe_ref[...]
    h = _rmsnorm(x, g_ref[...]).astype(BF16)
    gate = _sigmoid(_dot(h, wg_ref[...]))
    out = x + gate * _dot(p_ref[...].astype(BF16), wp_ref[...])
    o_ref[...] = out
    if maybe_y_ref:
        maybe_y_ref[0][...] = _rmsnorm(out, gf_ref[...])


def _ple(x, moe, p, g, w_gate, w_proj, g_final, final):
    t = x.shape[0]
    tm = TOKEN_TILE
    row = lambda width: pl.BlockSpec((tm, width), lambda i: (i, 0))
    par = lambda shape: pl.BlockSpec(shape, lambda i: (0,) * len(shape))
    n_out = 2 if final else 1
    return pl.pallas_call(
        _ple_kernel,
        grid=(t // tm,),
        in_specs=[row(D_MODEL), row(D_MODEL), row(PLE_DIM), par((1, D_MODEL)),
                  par((D_MODEL, D_MODEL)), par((PLE_DIM, D_MODEL)), par((1, D_MODEL))],
        out_specs=[row(D_MODEL)] * n_out,
        out_shape=[jax.ShapeDtypeStruct((t, D_MODEL), F32)] * n_out,
        compiler_params=_cparams(("parallel",)),
        name="ple",
    )(x, moe, p, g, w_gate, w_proj, g_final)


def _pad_rows(a, rows):
    return jnp.pad(a, ((0, 0), (rows - a.shape[1], 0), (0, 0)))


def _layer(i, x, p, cache_k, cache_v, page_table, state_ssm, state_ssm_conv, state_cf_conv,
           prm, dims, final):
    bp, sp_len, bs, ss_len = dims
    tp = bp * sp_len
    ts = bs * ss_len
    row = lambda v: v.reshape(1, -1)

    w_t = jnp.transpose(prm['w_in'][i])
    dt_rows = jnp.pad(w_t[OFF_DT:OFF_GLU], ((0, LANES - SSM_HEADS), (0, 0)))
    w_main_t = jnp.concatenate([w_t[:OFF_DT], w_t[OFF_GLU:OFF_GATE], dt_rows],
                               axis=0).astype(BF16)
    w_gates_t = w_t[OFF_GATE:].astype(BF16)
    g_mix = row(prm['norm_mix'][i])

    qb_p, kt_p, vt_p, ktb_p, vtb_p, z_p, xbc_p, glu_p, dt_p = _in_proj_prompt(
        x, g_mix, w_main_t, bp, sp_len)
    qb_s, k_s, v_s, kb_s, vb_s, z_s, xbc_s, glu_s, dt_s = _in_proj_rows(
        x, g_mix, w_main_t, tp, ts)

    a_prompt = _attn_prompt(qb_p, ktb_p, vtb_p, prm['sb_bias'][i], bp, sp_len)
    seq3 = lambda a: a.reshape(bs, ss_len, MIX_WIDTH)
    a_sample = _attn_decode(seq3(qb_s), seq3(kb_s), seq3(vb_s), cache_k, cache_v, page_table,
                            prm['sb_bias'][i], i).reshape(ts, MIX_WIDTH)

    pad_lanes = lambda vec, fill: jnp.pad(vec, (0, LANES - vec.shape[0]),
                                          constant_values=fill).reshape(1, LANES)
    ssd_prm = (prm['ssm_conv_w'][i], row(prm['ssm_conv_b'][i]), pad_lanes(prm['dt_bias'][i], 0.0),
               pad_lanes(-jnp.exp(prm['a_log'][i]), 0.0),
               row(jnp.repeat(prm['d_skip'][i], SSM_HEAD_DIM)), row(prm['ssm_norm'][i]))
    zeros_conv = jnp.zeros((bp, SUBLANES, SSM_CONV_CH), F32)
    zeros_h = jnp.zeros((bp, SSM_HEADS, SSM_HEAD_DIM, SSM_STATE), F32)
    b_prompt, sconv_p, h_p = _ssd(xbc_p, dt_p, z_p, zeros_conv, zeros_h, *ssd_prm,
                                  batch=bp, seq=sp_len, valid=sp_len)
    padseq = lambda a: jnp.pad(a.reshape(bs, ss_len, -1),
                               ((0, 0), (0, SSM_CHUNK - ss_len), (0, 0))).reshape(bs * SSM_CHUNK, -1)
    b_s, sconv_s, h_s = _ssd(padseq(xbc_s), padseq(dt_s), padseq(z_s),
                             _pad_rows(state_ssm_conv[i], SUBLANES), state_ssm[i], *ssd_prm,
                             batch=bs, seq=SSM_CHUNK, valid=ss_len)
    b_sample = b_s.reshape(bs, SSM_CHUNK, MIX_WIDTH)[:, :ss_len].reshape(ts, MIX_WIDTH)

    cf_prm = (prm['cf_conv_w'][i], row(prm['cf_conv_b'][i]), row(prm['cf_ln_g'][i]),
              row(prm['cf_ln_b'][i]))
    c_prompt, cf_p = _conformer(glu_p, jnp.zeros((bp, CF_HALO, MIX_WIDTH), F32), *cf_prm,
                                batch=bp, seq=sp_len)
    c_sample, cf_s = _conformer(glu_s, _pad_rows(state_cf_conv[i], CF_HALO), *cf_prm,
                                batch=bs, seq=ss_len)

    x = _merge(x, (a_prompt, b_prompt, c_prompt), (a_sample, b_sample, c_sample), g_mix, w_gates_t,
               prm['w_branch'][i].astype(BF16), prm['w_out'][i].astype(BF16))

    w_router = jnp.concatenate(
        [prm['w_router_expert'][i].reshape(D_MODEL, N_EXPERTS), prm['w_router_group'][i],
         jnp.zeros((D_MODEL, LANES - N_EXPERTS - MOE_GROUPS), F32)], axis=1)
    b_router = jnp.concatenate(
        [prm['b_router_expert'][i].reshape(N_EXPERTS), prm['b_router_group'][i],
         jnp.zeros((LANES - N_EXPERTS - MOE_GROUPS,), F32)]).reshape(1, LANES)
    tx, counts = _router(x, row(prm['norm_ffn'][i]), w_router, b_router)
    tile_group, n_tiles, src, dst = _moe_plan(tx, counts, MOE_ROW_TILE)
    grouped = lambda w: w.astype(BF16).reshape((MOE_GROUPS, EXPERTS_PER_GROUP) + w.shape[1:])
    moe = _experts(tx, tile_group, n_tiles, src, dst, grouped(prm['w_expert_gate'][i]),
                   grouped(prm['w_expert_up'][i]), grouped(prm['w_expert_down'][i]),
                   out_rows=src.size)

    outs = _ple(x, moe, p, row(prm['norm_ple'][i]), prm['w_ple_gate'][i].astype(BF16),
                prm['w_ple_proj'][i].astype(BF16), row(prm['norm_final']), final)

    conv_tail = SSM_CONV - 1
    cf_tail = CF_CONV - 1
    states = dict(
        k_p=jnp.transpose(kt_p.reshape(bp, SB_HEADS, SB_HEAD_DIM, sp_len), (0, 3, 1, 2)),
        v_p=jnp.transpose(vt_p.reshape(bp, SB_HEADS, SB_HEAD_DIM, sp_len), (0, 3, 1, 2)),
        h_p=h_p, sconv_p=sconv_p[:, SUBLANES - conv_tail:], cf_p=cf_p[:, CF_HALO - cf_tail:],
        k_s=k_s.reshape(bs, ss_len, SB_HEADS, SB_HEAD_DIM),
        v_s=v_s.reshape(bs, ss_len, SB_HEADS, SB_HEAD_DIM),
        h_s=h_s, sconv_s=sconv_s[:, SUBLANES - conv_tail:], cf_s=cf_s[:, CF_HALO - cf_tail:])
    return outs, states


def kernel(x_prompt, x_sample, p_prompt, p_sample, cache_k, cache_v, page_table, state_ssm, state_ssm_conv, state_cf_conv, norm_mix, w_in, sb_bias, ssm_conv_w, ssm_conv_b, dt_bias, a_log, d_skip, ssm_norm, cf_conv_w, cf_conv_b, cf_ln_g, cf_ln_b, w_branch, w_out, norm_ffn, w_router_group, b_router_group, w_router_expert, b_router_expert, w_expert_gate, w_expert_up, w_expert_down, norm_ple, w_ple_gate, w_ple_proj, norm_final):
    prm = dict(norm_mix=norm_mix, w_in=w_in, sb_bias=sb_bias, ssm_conv_w=ssm_conv_w,
               ssm_conv_b=ssm_conv_b, dt_bias=dt_bias, a_log=a_log, d_skip=d_skip,
               ssm_norm=ssm_norm, cf_conv_w=cf_conv_w, cf_conv_b=cf_conv_b, cf_ln_g=cf_ln_g,
               cf_ln_b=cf_ln_b, w_branch=w_branch, w_out=w_out, norm_ffn=norm_ffn,
               w_router_group=w_router_group, b_router_group=b_router_group,
               w_router_expert=w_router_expert, b_router_expert=b_router_expert,
               w_expert_gate=w_expert_gate, w_expert_up=w_expert_up,
               w_expert_down=w_expert_down, norm_ple=norm_ple, w_ple_gate=w_ple_gate,
               w_ple_proj=w_ple_proj, norm_final=norm_final)
    depth = w_in.shape[0]
    bp, sp_len, _ = x_prompt.shape
    bs, ss_len, _ = x_sample.shape
    tp = bp * sp_len
    ts = bs * ss_len
    dims = (bp, sp_len, bs, ss_len)
    n_pool = cache_k.shape[1]
    ck = jnp.transpose(cache_k, (0, 1, 3, 4, 2)).reshape(depth, n_pool, MIX_WIDTH, PAGE_SIZE)
    cv = jnp.transpose(cache_v, (0, 1, 3, 4, 2)).reshape(depth, n_pool, MIX_WIDTH, PAGE_SIZE)

    x = jnp.concatenate([x_prompt.reshape(tp, D_MODEL), x_sample.reshape(ts, D_MODEL)], axis=0)
    per_layer = []
    for i in range(depth):
        p = jnp.concatenate([p_prompt[i].reshape(tp, PLE_DIM), p_sample[i].reshape(ts, PLE_DIM)],
                            axis=0)
        outs, st = _layer(i, x, p, ck, cv, page_table, state_ssm, state_ssm_conv, state_cf_conv,
                          prm, dims, final=(i == depth - 1))
        x = outs[0]
        per_layer.append(st)
    y = outs[1]
    stack = lambda name: jnp.stack([st[name] for st in per_layer])
    return (y[:tp].reshape(bp, sp_len, D_MODEL), y[tp:].reshape(bs, ss_len, D_MODEL),
            stack('k_p'), stack('v_p'), stack('h_p'), stack('sconv_p'), stack('cf_p'),
            stack('k_s'), stack('v_s'), stack('h_s'), stack('sconv_s'), stack('cf_s'))
```

```python
import functools

import jax
import jax.numpy as jnp
from jax import lax
from jax.experimental import pallas as pl
from jax.experimental.pallas import tpu as pltpu

F32 = jnp.float32
BF16 = jnp.bfloat16
EPS = 1e-6

D_MODEL = 1024
MIX_WIDTH = 512
N_BRANCH = 3
SB_HEADS = 8
SB_HEAD_DIM = 64
SSM_HEADS = 8
SSM_HEAD_DIM = 64
SSM_GROUPS = 2
SSM_STATE = 128
SSM_CONV = 4
SSM_CONV_CH = MIX_WIDTH + 2 * SSM_GROUPS * SSM_STATE
SSM_CHUNK = 128
CF_CONV = 31
MOE_GROUPS = 4
EXPERTS_PER_GROUP = 4
N_EXPERTS = MOE_GROUPS * EXPERTS_PER_GROUP
D_EXPERT = 512
PLE_DIM = 256
PAGE_SIZE = 128
OFF_DT = 3 * MIX_WIDTH + MIX_WIDTH + SSM_CONV_CH
OFF_GLU = OFF_DT + SSM_HEADS
OFF_GATE = OFF_GLU + 2 * MIX_WIDTH

LANES = 128
SUBLANES = 8
VMEM_LIMIT = 56 * 1024 * 1024

TOKEN_TILE = 256
MOE_ROW_TILE = 256
ATTN_TILE = 256
CF_TILE = 256
DEC_PAGES_PER_STEP = 16


def _cparams(sem):
    return pltpu.CompilerParams(dimension_semantics=sem, vmem_limit_bytes=VMEM_LIMIT)


def _rmsnorm(x, g):
    return x * lax.rsqrt(jnp.mean(x * x, axis=-1, keepdims=True) + EPS) * g


def _sigmoid(x):
    return 1.0 / (1.0 + jnp.exp(-x))


def _silu(x):
    return x * _sigmoid(x)


def _softplus(x):
    return jnp.maximum(x, 0.0) + jnp.log(1.0 + jnp.exp(-jnp.abs(x)))


LOG2E = 1.4426950408889634


def _softplus2(x):
    return jnp.maximum(x, 0.0) + jnp.log2(1.0 + jnp.exp2(-jnp.abs(x)))


def _dot(a, b):
    return jnp.dot(a, b, preferred_element_type=F32)


def _dot_nt(a, b):
    return lax.dot_general(a, b, (((1,), (1,)), ((), ())), preferred_element_type=F32)


def _dot_tn(a, b):
    return lax.dot_general(a, b, (((0,), (0,)), ((), ())), preferred_element_type=F32)


def _split3(x):
    hi = x.astype(BF16)
    r = x - hi.astype(F32)
    mid = r.astype(BF16)
    lo = (r - mid.astype(F32)).astype(BF16)
    return hi, mid, lo


def _dot_exact_lhs(mat_bf16, x):
    hi, mid, lo = _split3(x)
    return _dot(mat_bf16, hi) + _dot(mat_bf16, mid) + _dot(mat_bf16, lo)


IN_PROJ_WIDTHS = [MIX_WIDTH, MIX_WIDTH, MIX_WIDTH, MIX_WIDTH, MIX_WIDTH, MIX_WIDTH,
                  SSM_CONV_CH, 2 * MIX_WIDTH, LANES]
IN_PROJ_DTYPES = [BF16, F32, F32, BF16, BF16, F32, F32, F32, F32]


def _in_proj_kernel(x_ref, g_ref, w_ref, q_ref, k_ref, v_ref, kb_ref, vb_ref,
                    z_ref, xbc_ref, glu_ref, dt_ref, *, transposed_kv):
    h = _rmsnorm(x_ref[...], g_ref[...]).astype(BF16)
    w = MIX_WIDTH
    if transposed_kv:
        q = _dot_nt(h, w_ref[0:w, :])
        kv = _dot_nt(w_ref[w:3 * w, :], h)
        k = kv[0:w]
        v = kv[w:2 * w]
        rest = _dot_nt(h, w_ref[3 * w:, :])
    else:
        p = _dot_nt(h, w_ref[...])
        q = p[:, 0:w]
        k = p[:, w:2 * w]
        v = p[:, 2 * w:3 * w]
        rest = p[:, 3 * w:]
    q_ref[...] = (q * (SB_HEAD_DIM ** -0.5 * LOG2E)).astype(BF16)
    k_ref[...] = k
    v_ref[...] = v
    kb_ref[...] = k.astype(BF16)
    vb_ref[...] = v.astype(BF16)
    z_ref[...] = rest[:, 0:w]
    xbc_ref[...] = rest[:, w:3 * w]
    glu_ref[...] = rest[:, 3 * w:5 * w]
    dt_ref[...] = rest[:, 5 * w:5 * w + LANES]


def _in_proj_prompt(x, g, w_main_t, batch, seq):
    tm = TOKEN_TILE
    nt = seq // tm
    n = w_main_t.shape[0]
    row = lambda width: pl.BlockSpec((tm, width), lambda b, j: (b * nt + j, 0))
    par = lambda shape: pl.BlockSpec(shape, lambda b, j: (0,) * len(shape))
    out_specs = [row(wd) for wd in IN_PROJ_WIDTHS]
    out_shape = [jax.ShapeDtypeStruct((batch * seq, wd), dt)
                 for wd, dt in zip(IN_PROJ_WIDTHS, IN_PROJ_DTYPES)]
    for idx in (1, 2):
        out_specs[idx] = pl.BlockSpec((None, MIX_WIDTH, tm), lambda b, j: (b, 0, j))
        out_shape[idx] = jax.ShapeDtypeStruct((batch, MIX_WIDTH, seq), F32)
    for idx in (3, 4):
        out_specs[idx] = pl.BlockSpec((None, None, MIX_WIDTH, tm), lambda b, j: (b, j, 0, 0))
        out_shape[idx] = jax.ShapeDtypeStruct((batch, nt, MIX_WIDTH, tm), BF16)
    return pl.pallas_call(
        functools.partial(_in_proj_kernel, transposed_kv=True),
        grid=(batch, nt),
        in_specs=[row(D_MODEL), par((1, D_MODEL)), par((n, D_MODEL))],
        out_specs=out_specs,
        out_shape=out_shape,
        compiler_params=_cparams(("parallel", "parallel")),
        name="in_proj_prompt",
    )(x, g, w_main_t)


def _in_proj_rows(x, g, w_main_t, row_start, rows):
    tm = TOKEN_TILE
    first = row_start // tm
    n = w_main_t.shape[0]
    par = lambda shape: pl.BlockSpec(shape, lambda i: (0,) * len(shape))
    return pl.pallas_call(
        functools.partial(_in_proj_kernel, transposed_kv=False),
        grid=(rows // tm,),
        in_specs=[pl.BlockSpec((tm, D_MODEL), lambda i: (i + first, 0)),
                  par((1, D_MODEL)), par((n, D_MODEL))],
        out_specs=[pl.BlockSpec((tm, wd), lambda i: (i, 0)) for wd in IN_PROJ_WIDTHS],
        out_shape=[jax.ShapeDtypeStruct((rows, wd), dt)
                   for wd, dt in zip(IN_PROJ_WIDTHS, IN_PROJ_DTYPES)],
        compiler_params=_cparams(("parallel",)),
        name="in_proj_rows",
    )(x, g, w_main_t)


def _strict_upper(n):
    r = lax.broadcasted_iota(jnp.int32, (n, n), 0)
    c = lax.broadcasted_iota(jnp.int32, (n, n), 1)
    return jnp.where(r > c, 1.0, 0.0).astype(BF16)


def _sb_block(z, v_blk, upper, carry, mask, v_channel_major=False):
    sp = _softplus2(z)
    if mask is not None:
        sp = jnp.where(mask, sp, 0.0)
    after = _dot(sp.astype(BF16), upper)
    w = jnp.exp2(z - sp - after - carry)
    if mask is not None:
        w = jnp.where(mask, w, 0.0)
    pv = _dot_nt if v_channel_major else _dot
    return pv(w.astype(BF16), v_blk), jnp.sum(sp, axis=1, keepdims=True)


def _attn_prompt_kernel(bias_ref, q_ref, k_ref, v_ref, o_ref, acc_ref, carry_ref, *, seq, tile):
    hp = pl.program_id(1)
    n_q = seq // tile
    upper = _strict_upper(tile)
    lane = lax.broadcasted_iota(jnp.int32, (tile, LANES), 1)
    head0 = lane < SB_HEAD_DIM
    r = lax.broadcasted_iota(jnp.int32, (tile, tile), 0)
    c = lax.broadcasted_iota(jnp.int32, (tile, tile), 1)
    diag_mask = jnp.concatenate([c < r, c < r], axis=0)
    bias0 = bias_ref[hp * 2] * LOG2E
    bias1 = bias_ref[hp * 2 + 1] * LOG2E

    def visit(qis, kj, mask):
        zs, sps, afters, carries = [], [], [], []
        for qi in qis:
            q = q_ref[pl.ds(pl.multiple_of(qi * tile, tile), tile), :]
            zero = jnp.zeros_like(q)
            q2 = jnp.concatenate([jnp.where(head0, q, zero), jnp.where(head0, zero, q)], axis=0)
            d = _dot(q2, k_ref[kj])
            zs.append(jnp.concatenate([d[:tile] + bias0, d[tile:] + bias1], axis=0))
        for z in zs:
            sp = _softplus2(z)
            sps.append(sp if mask is None else jnp.where(mask, sp, 0.0))
        for sp in sps:
            afters.append(_dot(sp.astype(BF16), upper))
        contribs = []
        for qi, z, sp, after in zip(qis, zs, sps, afters):
            if mask is None:
                carry = carry_ref[qi]
                carries.append(carry)
                after = after + jnp.concatenate([carry] * (tile // LANES), axis=1)
            w = jnp.exp2(z - sp - after)
            if mask is not None:
                w = jnp.where(mask, w, 0.0)
            contribs.append(_dot_nt(w.astype(BF16), v_ref[kj]))
        for n, (qi, sp, contrib) in enumerate(zip(qis, sps, contribs)):
            row_sum = jnp.broadcast_to(jnp.sum(sp, axis=1, keepdims=True), (2 * tile, LANES))
            if mask is None:
                acc_ref[qi] += contrib
                carry_ref[qi] = carries[n] + row_sum
            else:
                acc_ref[qi] = contrib
                carry_ref[qi] = row_sum

    def key_tile(jj, _):
        kj = n_q - 1 - jj
        visit([kj], kj, diag_mask)

        def q_pair(i, _):
            visit([kj + 1 + 2 * i, kj + 2 + 2 * i], kj, None)
            return 0

        lax.fori_loop(0, jj // 2, q_pair, 0)

        @pl.when(jj % 2 == 1)
        def _():
            visit([n_q - 1], kj, None)

        return 0

    lax.fori_loop(0, n_q, key_tile, 0)

    def write(qi, _):
        acc = acc_ref[qi]
        o_ref[pl.ds(pl.multiple_of(qi * tile, tile), tile), :] = jnp.where(
            head0, acc[:tile], acc[tile:])
        return 0

    lax.fori_loop(0, n_q, write, 0)


def _attn_prompt(q, kt, vt, sb_bias, batch, seq):
    tile = kt.shape[-1]
    nt = seq // tile
    blk = pl.BlockSpec((seq, LANES), lambda b, hp: (b, hp))
    kv_blk = pl.BlockSpec((None, nt, LANES, tile), lambda b, hp: (b, 0, hp, 0))
    return pl.pallas_call(
        functools.partial(_attn_prompt_kernel, seq=seq, tile=tile),
        grid=(batch, SB_HEADS // 2),
        in_specs=[pl.BlockSpec(memory_space=pltpu.SMEM), blk, kv_blk, kv_blk],
        out_specs=blk,
        out_shape=jax.ShapeDtypeStruct((batch * seq, MIX_WIDTH), F32),
        scratch_shapes=[pltpu.VMEM((nt, 2 * tile, LANES), F32),
                        pltpu.VMEM((nt, 2 * tile, LANES), F32)],
        compiler_params=_cparams(("parallel", "parallel")),
        name="attn_prompt",
    )(sb_bias, q, kt, vt)


def _attn_decode_kernel(pt_ref, bias_ref, q_ref, kn_ref, vn_ref, *rest, pages, dec_seq):
    k_refs = rest[:pages]
    v_refs = rest[pages:2 * pages]
    o_ref = rest[2 * pages]
    acc_ref, carry_ref = rest[2 * pages + 1:]
    step = pl.program_id(1)
    rows = SB_HEADS * dec_seq
    width = MIX_WIDTH
    upper = _strict_upper(PAGE_SIZE)
    r_w = lax.broadcasted_iota(jnp.int32, (rows, width), 0)
    c_w = lax.broadcasted_iota(jnp.int32, (rows, width), 1)
    head_mask = (r_w // dec_seq) == (c_w // SB_HEAD_DIM)
    q_rep = jnp.concatenate([q_ref[0].astype(F32)] * SB_HEADS, axis=0)
    q_bd = jnp.where(head_mask, q_rep, 0.0).astype(BF16)
    r_k = lax.broadcasted_iota(jnp.int32, (rows, PAGE_SIZE), 0)
    c_k = lax.broadcasted_iota(jnp.int32, (rows, PAGE_SIZE), 1)
    bias = jnp.zeros((rows, PAGE_SIZE), F32)
    for h in range(SB_HEADS):
        bias = jnp.where(r_k // dec_seq == h, bias_ref[h] * LOG2E, bias)

    @pl.when(step == 0)
    def _():
        pad = jnp.zeros((PAGE_SIZE - dec_seq, width), F32)
        kn = jnp.concatenate([kn_ref[0].astype(F32), pad], axis=0).astype(BF16)
        vn = jnp.concatenate([vn_ref[0].astype(F32), pad], axis=0).astype(BF16)
        z = _dot_nt(q_bd, kn) + bias
        mask = c_k < (r_k % dec_seq)
        contrib, rs = _sb_block(z, vn, upper, 0.0, mask)
        acc_ref[...] = contrib
        carry_ref[...] = jnp.broadcast_to(rs, (rows, PAGE_SIZE))

    k_all = jnp.concatenate([k_refs[j][...].astype(BF16) for j in range(pages)], axis=1)
    z = _dot(q_bd, k_all)
    z = jnp.concatenate([z[:, j * PAGE_SIZE:(j + 1) * PAGE_SIZE] + bias for j in range(pages)],
                        axis=0)
    sp = _softplus2(z)
    after = _dot(sp.astype(BF16), upper)
    row_sum = jnp.sum(sp, axis=1, keepdims=True)
    carry = carry_ref[...]
    carries = []
    for j in range(pages):
        carries.append(carry)
        carry = carry + row_sum[j * rows:(j + 1) * rows]
    w = jnp.exp2(z - sp - after - jnp.concatenate(carries, axis=0)).astype(BF16)
    w_all = jnp.concatenate([w[j * rows:(j + 1) * rows] for j in range(pages)], axis=1)
    v_all = jnp.concatenate([v_refs[j][...].astype(BF16) for j in range(pages)], axis=1)
    acc = acc_ref[...] + _dot_nt(w_all, v_all)
    acc_ref[...] = acc
    carry_ref[...] = carry

    @pl.when(step == pl.num_programs(1) - 1)
    def _():
        a = jnp.where(head_mask, acc, 0.0)
        out = a[0:dec_seq]
        for h in range(1, SB_HEADS):
            out = out + a[h * dec_seq:(h + 1) * dec_seq]
        o_ref[0] = out


def _attn_decode(q, k_new, v_new, cache_k, cache_v, page_table, sb_bias, layer):
    bsz, dec_seq, width = q.shape
    n_pages = page_table.shape[1]
    pages = DEC_PAGES_PER_STEP
    assert n_pages % pages == 0 and dec_seq == SUBLANES
    steps = n_pages // pages
    rows = SB_HEADS * dec_seq

    def page_spec(j):
        def imap(b, s, pt):
            return (layer, pt[b * n_pages + (n_pages - 1 - (s * pages + j))], 0, 0)
        return pl.BlockSpec((None, None, width, PAGE_SIZE), imap)

    seq_spec = pl.BlockSpec((1, dec_seq, width), lambda b, s, pt: (b, 0, 0))
    grid_spec = pltpu.PrefetchScalarGridSpec(
        num_scalar_prefetch=1,
        grid=(bsz, steps),
        in_specs=[pl.BlockSpec(memory_space=pltpu.SMEM), seq_spec, seq_spec, seq_spec]
        + [page_spec(j) for j in range(pages)] * 2,
        out_specs=seq_spec,
        scratch_shapes=[pltpu.VMEM((rows, width), F32), pltpu.VMEM((rows, PAGE_SIZE), F32)],
    )
    return pl.pallas_call(
        functools.partial(_attn_decode_kernel, pages=pages, dec_seq=dec_seq),
        grid_spec=grid_spec,
        out_shape=jax.ShapeDtypeStruct((bsz, dec_seq, width), F32),
        compiler_params=_cparams(("parallel", "arbitrary")),
        name="attn_decode",
    )(page_table.reshape(-1), sb_bias, q, k_new, v_new,
      *([cache_k] * pages), *([cache_v] * pages))


def _expand_heads(cols, width):
    m = cols.shape[0]
    lane = lax.broadcasted_iota(jnp.int32, (m, width), 1)
    out = jnp.zeros((m, width), F32)
    for h in range(width // SSM_HEAD_DIM):
        out = jnp.where(lane // SSM_HEAD_DIM == h, cols[:, h:h + 1], out)
    return out


def _ssd_kernel(xbc_ref, dt_ref, z_ref, buf_ref, h0_ref, cw_ref, cb_ref, dtb_ref, a_ref,
                dskip_ref, ng_ref, y_ref, conv_ref, h_ref, xp_ref, hs_ref, *, chunk, valid):
    c = pl.program_id(1)
    q = chunk
    halo = SUBLANES

    @pl.when(c == 0)
    def _():
        xp_ref[0:halo, :] = buf_ref[0]
        hs_ref[...] = h0_ref[0]

    xp_ref[halo:halo + q, :] = xbc_ref[...]
    acc = cb_ref[...] + cw_ref[SSM_CONV - 1:SSM_CONV, :] * xp_ref[halo:halo + q, :]
    for j in range(SSM_CONV - 1):
        off = halo - (SSM_CONV - 1) + j
        acc = acc + cw_ref[j:j + 1, :] * xp_ref[off:off + q, :]
    last = min(valid, q)
    tail = xp_ref[last:last + halo, :]
    conv_ref[0] = tail
    xp_ref[0:halo, :] = tail
    xc = _silu(acc)
    xs = xc[:, 0:MIX_WIDTH]
    gw = SSM_GROUPS * SSM_STATE
    bm = xc[:, MIX_WIDTH:MIX_WIDTH + gw]
    cm = xc[:, MIX_WIDTH + gw:MIX_WIDTH + 2 * gw]

    dt = _softplus(dt_ref[...] + dtb_ref[...])
    if valid < q:
        row = lax.broadcasted_iota(jnp.int32, (q, LANES), 0)
        dt = jnp.where(row < valid, dt, 0.0)
    da = dt * a_ref[...]
    r = lax.broadcasted_iota(jnp.int32, (q, q), 0)
    s = lax.broadcasted_iota(jnp.int32, (q, q), 1)
    causal = s <= r
    tril = jnp.where(causal, 1.0, 0.0).astype(BF16)
    cum = _dot_exact_lhs(tril, da)
    cum_t = jnp.transpose(cum)
    dt_t = jnp.transpose(dt)
    cum_last = cum[q - 1:q, :]
    w_end = jnp.exp(cum_last - cum) * dt
    e_cum = jnp.exp(cum)

    hpg = SSM_HEADS // SSM_GROUPS
    gwid = hpg * SSM_HEAD_DIM
    lane_g = lax.broadcasted_iota(jnp.int32, (q, gwid), 1)
    xs_b = xs.astype(BF16)
    e_cum_x = _expand_heads(e_cum, MIX_WIDTH)
    w_end_x = _expand_heads(w_end, MIX_WIDTH)
    xw = (xs * w_end_x).astype(BF16)
    y_parts = []
    for g in range(SSM_GROUPS):
        b_g = bm[:, g * SSM_STATE:(g + 1) * SSM_STATE].astype(BF16)
        c_g = cm[:, g * SSM_STATE:(g + 1) * SSM_STATE].astype(BF16)
        cb = _dot_nt(c_g, b_g)
        xs_g = xs_b[:, g * gwid:(g + 1) * gwid]
        y_g = jnp.zeros((q, gwid), F32)
        for e in range(hpg):
            hd = g * hpg + e
            seg = cum[:, hd:hd + 1] - cum_t[hd:hd + 1, :]
            decay = jnp.where(causal, jnp.exp(jnp.where(causal, seg, 0.0)), 0.0)
            w = (cb * decay * dt_t[hd:hd + 1, :]).astype(BF16)
            y_g = jnp.where(lane_g // SSM_HEAD_DIM == e, _dot(w, xs_g), y_g)
        h_g = hs_ref[g * hpg:(g + 1) * hpg].reshape(gwid, SSM_STATE)
        y_prev = _dot_nt(c_g, h_g.astype(BF16))
        y_parts.append(y_g + y_prev * e_cum_x[:, g * gwid:(g + 1) * gwid])
        upd = _dot_tn(xw[:, g * gwid:(g + 1) * gwid], b_g)
        for e in range(hpg):
            hd = g * hpg + e
            scale = jnp.exp(cum_last[:, hd:hd + 1])
            hs_ref[hd] = hs_ref[hd] * scale + upd[e * SSM_HEAD_DIM:(e + 1) * SSM_HEAD_DIM, :]
    y = jnp.concatenate(y_parts, axis=1) + dskip_ref[...] * xs
    y = y * _silu(z_ref[...])
    y_ref[...] = _rmsnorm(y, ng_ref[...])
    h_ref[0] = hs_ref[...]


def _ssd(xbc, dt_raw, z, conv_buf, h0, conv_w, conv_b, dt_bias, a_neg, d_skip_row, norm_g,
         batch, seq, valid):
    q = SSM_CHUNK
    nc = seq // q
    row = lambda width: pl.BlockSpec((q, width), lambda b, c: (b * nc + c, 0))
    par = lambda shape: pl.BlockSpec(shape, lambda b, c: (0,) * len(shape))
    return pl.pallas_call(
        functools.partial(_ssd_kernel, chunk=q, valid=valid),
        grid=(batch, nc),
        in_specs=[row(SSM_CONV_CH), row(LANES), row(MIX_WIDTH),
                  pl.BlockSpec((1, SUBLANES, SSM_CONV_CH), lambda b, c: (b, 0, 0)),
                  pl.BlockSpec((1, SSM_HEADS, SSM_HEAD_DIM, SSM_STATE), lambda b, c: (b, 0, 0, 0)),
                  par((SSM_CONV, SSM_CONV_CH)), par((1, SSM_CONV_CH)), par((1, LANES)),
                  par((1, LANES)), par((1, MIX_WIDTH)), par((1, MIX_WIDTH))],
        out_specs=[row(MIX_WIDTH),
                   pl.BlockSpec((1, SUBLANES, SSM_CONV_CH), lambda b, c: (b, 0, 0)),
                   pl.BlockSpec((1, SSM_HEADS, SSM_HEAD_DIM, SSM_STATE), lambda b, c: (b, 0, 0, 0))],
        out_shape=[jax.ShapeDtypeStruct((batch * seq, MIX_WIDTH), F32),
                   jax.ShapeDtypeStruct((batch, SUBLANES, SSM_CONV_CH), F32),
                   jax.ShapeDtypeStruct((batch, SSM_HEADS, SSM_HEAD_DIM, SSM_STATE), F32)],
        scratch_shapes=[pltpu.VMEM((q + 2 * SUBLANES, SSM_CONV_CH), F32),
                        pltpu.VMEM((SSM_HEADS, SSM_HEAD_DIM, SSM_STATE), F32)],
        compiler_params=_cparams(("parallel", "arbitrary")),
        name="ssd",
    )(xbc, dt_raw, z, conv_buf, h0, conv_w, conv_b, dt_bias, a_neg, d_skip_row, norm_g)


CF_HALO = 32


def _conformer_kernel(glu_ref, buf_ref, cw_ref, cb_ref, lg_ref, lb_ref, y_ref, new_ref,
                      xp_ref, *, tile):
    c = pl.program_id(1)

    @pl.when(c == 0)
    def _():
        xp_ref[0:CF_HALO, :] = buf_ref[0]

    glu = glu_ref[...]
    u = glu[:, 0:MIX_WIDTH] * _sigmoid(glu[:, MIX_WIDTH:2 * MIX_WIDTH])
    xp_ref[CF_HALO:CF_HALO + tile, :] = u
    acc = cb_ref[...] + cw_ref[CF_CONV - 1:CF_CONV, :] * u
    for j in range(CF_CONV - 1):
        off = CF_HALO - (CF_CONV - 1) + j
        acc = acc + cw_ref[j:j + 1, :] * xp_ref[off:off + tile, :]
    tail = xp_ref[tile:tile + CF_HALO, :]
    new_ref[0] = tail
    xp_ref[0:CF_HALO, :] = tail
    mu = jnp.mean(acc, axis=-1, keepdims=True)
    xc = acc - mu
    var = jnp.mean(xc * xc, axis=-1, keepdims=True)
    y_ref[...] = _silu(xc * lax.rsqrt(var + EPS) * lg_ref[...] + lb_ref[...])


def _conformer(glu, conv_buf, conv_w, conv_b, ln_g, ln_b, batch, seq):
    tile = CF_TILE if seq % CF_TILE == 0 else seq
    nt = seq // tile
    par = lambda shape: pl.BlockSpec(shape, lambda b, c: (0,) * len(shape))
    buf_spec = pl.BlockSpec((1, CF_HALO, MIX_WIDTH), lambda b, c: (b, 0, 0))
    return pl.pallas_call(
        functools.partial(_conformer_kernel, tile=tile),
        grid=(batch, nt),
        in_specs=[pl.BlockSpec((tile, 2 * MIX_WIDTH), lambda b, c: (b * nt + c, 0)), buf_spec,
                  par((CF_CONV, MIX_WIDTH)), par((1, MIX_WIDTH)), par((1, MIX_WIDTH)),
                  par((1, MIX_WIDTH))],
        out_specs=[pl.BlockSpec((tile, MIX_WIDTH), lambda b, c: (b * nt + c, 0)), buf_spec],
        out_shape=[jax.ShapeDtypeStruct((batch * seq, MIX_WIDTH), F32),
                   jax.ShapeDtypeStruct((batch, CF_HALO, MIX_WIDTH), F32)],
        scratch_shapes=[pltpu.VMEM((tile + CF_HALO, MIX_WIDTH), F32)],
        compiler_params=_cparams(("parallel", "arbitrary")),
        name="conformer",
    )(glu, conv_buf, conv_w, conv_b, ln_g, ln_b)


def _merge_kernel(xp_ref, xs_ref, ap_ref, bp_ref, cp_ref, as_ref, bs_ref, cs_ref, g_ref, wg_ref,
                  wb_ref, wo_ref, o_ref, *, prompt_tiles):
    is_prompt = pl.program_id(0) < prompt_tiles
    x = jnp.where(is_prompt, xp_ref[...], xs_ref[...])
    h = _rmsnorm(x, g_ref[...]).astype(BF16)
    merged = jnp.zeros(x.shape, F32)
    for n, (br_p, br_s) in enumerate(((ap_ref, as_ref), (bp_ref, bs_ref), (cp_ref, cs_ref))):
        br = jnp.where(is_prompt, br_p[...], br_s[...])
        gate = _sigmoid(_dot_nt(h, wg_ref[n * D_MODEL:(n + 1) * D_MODEL, :]))
        merged = merged + gate * _dot(br.astype(BF16), wb_ref[n])
    o_ref[...] = x + _dot(merged.astype(BF16), wo_ref[...])


def _merge(x_p, x_s, prompt_branches, sample_branches, g, w_gates, w_branch, w_out):
    tm = TOKEN_TILE
    prompt_tiles = x_p.shape[0] // tm
    assert x_s.shape[0] == tm and sample_branches[0].shape[0] == tm
    p_row = lambda width: pl.BlockSpec((tm, width),
                                       lambda i: (jnp.minimum(i, prompt_tiles - 1), 0))
    s_row = lambda width: pl.BlockSpec((tm, width), lambda i: (0, 0))
    par = lambda shape: pl.BlockSpec(shape, lambda i: (0,) * len(shape))
    return pl.pallas_call(
        functools.partial(_merge_kernel, prompt_tiles=prompt_tiles),
        grid=(prompt_tiles + 1,),
        in_specs=[p_row(D_MODEL), s_row(D_MODEL)] + [p_row(MIX_WIDTH)] * 3 + [s_row(MIX_WIDTH)] * 3
        + [par((1, D_MODEL)), par((N_BRANCH * D_MODEL, D_MODEL)),
           par((N_BRANCH, MIX_WIDTH, D_MODEL)), par((D_MODEL, D_MODEL))],
        out_specs=pl.BlockSpec((tm, D_MODEL), lambda i: (i, 0)),
        out_shape=jax.ShapeDtypeStruct(((prompt_tiles + 1) * tm, D_MODEL), F32),
        compiler_params=_cparams(("parallel",)),
        name="merge",
    )(x_p, x_s, *prompt_branches, *sample_branches, g, w_gates, w_branch, w_out)


META_GROUP_LANE = N_EXPERTS


def _router_kernel(x_ref, g_ref, wr_ref, br_ref, tx_ref, cnt_ref, run_ref):
    i = pl.program_id(0)

    @pl.when(i == 0)
    def _():
        run_ref[...] = jnp.zeros_like(run_ref)

    t = _rmsnorm(x_ref[...], g_ref[...])
    tx_ref[:, 0:D_MODEL] = t
    logits = jnp.dot(t, wr_ref[...], preferred_element_type=F32,
                     precision=lax.Precision.HIGHEST) + br_ref[...]
    lane = lax.broadcasted_iota(jnp.int32, logits.shape, 1)
    neg = -jnp.inf
    is_group = (lane >= N_EXPERTS) & (lane < N_EXPERTS + MOE_GROUPS)
    gl = jnp.where(is_group, logits, neg)
    gmax = jnp.max(gl, axis=1, keepdims=True)
    g_idx = jnp.min(jnp.where(gl == gmax, lane, LANES), axis=1, keepdims=True) - N_EXPERTS
    g_w = 1.0 / jnp.sum(jnp.where(is_group, jnp.exp(logits - gmax), 0.0), axis=1, keepdims=True)
    in_group = (lane // EXPERTS_PER_GROUP) == g_idx
    el = jnp.where(in_group, logits, neg)
    v1 = jnp.max(el, axis=1, keepdims=True)
    i1 = jnp.min(jnp.where(el == v1, lane, LANES), axis=1, keepdims=True)
    el2 = jnp.where(lane == i1, neg, el)
    v2 = jnp.max(el2, axis=1, keepdims=True)
    i2 = jnp.min(jnp.where(el2 == v2, lane, LANES), axis=1, keepdims=True)
    e21 = jnp.exp(v2 - v1)
    p1 = 1.0 / (1.0 + e21)
    p2 = e21 * p1
    comb = jnp.where(lane == i1, p1 * g_w, 0.0) + jnp.where(lane == i2, p2 * g_w, 0.0)
    onehot = jnp.where(lane == g_idx, 1.0, 0.0)
    run = run_ref[...] + jnp.sum(onehot, axis=0, keepdims=True)
    run_ref[...] = run
    cnt_ref[...] = run
    tx_ref[:, D_MODEL:] = jnp.where(lane == META_GROUP_LANE, g_idx.astype(F32), comb)


def _router(x, g, w_router, b_router):
    t = x.shape[0]
    tm = TOKEN_TILE
    row = lambda width: pl.BlockSpec((tm, width), lambda i: (i, 0))
    par = lambda shape: pl.BlockSpec(shape, lambda i: (0,) * len(shape))
    return pl.pallas_call(
        _router_kernel,
        grid=(t // tm,),
        in_specs=[row(D_MODEL), par((1, D_MODEL)), par((D_MODEL, LANES)), par((1, LANES))],
        out_specs=[row(D_MODEL + LANES), par((1, LANES))],
        out_shape=[jax.ShapeDtypeStruct((t, D_MODEL + LANES), F32),
                   jax.ShapeDtypeStruct((1, LANES), F32)],
        scratch_shapes=[pltpu.VMEM((1, LANES), F32)],
        compiler_params=_cparams(("arbitrary",)),
        name="router",
    )(x, g, w_router, b_router)


def _experts_kernel(tile_group_ref, src_ref, src_next_ref, dst_ref, dst_prev_ref, tx_hbm,
                    wg_ref, wu_ref, wd_ref, y_hbm, xs_ref, ys_ref, gsem, ssem, *, tm):
    i = pl.program_id(0)
    last_step = pl.num_programs(0) - 1
    slot = i % 2

    def start_gather(idx_ref, s):
        for r in range(tm):
            pltpu.make_async_copy(tx_hbm.at[pl.ds(idx_ref[0, 0, r], 1)],
                                  xs_ref.at[s, pl.ds(r, 1)], gsem.at[s]).start()

    def start_scatter(idx_ref, s):
        for r in range(tm):
            pltpu.make_async_copy(ys_ref.at[s, pl.ds(r, 1)],
                                  y_hbm.at[pl.ds(idx_ref[0, 0, r], 1)], ssem.at[s]).start()

    def wait_gather(s):
        pltpu.make_async_copy(tx_hbm.at[pl.ds(0, tm)], xs_ref.at[s], gsem.at[s]).wait()

    def wait_scatter(s):
        pltpu.make_async_copy(ys_ref.at[s], y_hbm.at[pl.ds(0, tm)], ssem.at[s]).wait()

    def step(first, last):
        if first:
            start_gather(src_ref, slot)
        wait_gather(slot)
        if not last:
            start_gather(src_next_ref, 1 - slot)
        if not first:
            start_scatter(dst_prev_ref, 1 - slot)
        rows = xs_ref[slot]
        x = rows[:, 0:D_MODEL].astype(BF16)
        meta = rows[:, D_MODEL:]
        lane = lax.broadcasted_iota(jnp.int32, meta.shape, 1)
        first_expert = tile_group_ref[i] * EXPERTS_PER_GROUP
        acc = jnp.zeros((tm, D_MODEL), F32)
        for e in range(EXPERTS_PER_GROUP):
            w_e = jnp.sum(jnp.where(lane == first_expert + e, meta, 0.0), axis=1, keepdims=True)
            hg = _dot(x, wg_ref[e])
            hu = _dot(x, wu_ref[e])
            act = (_silu(hg) * hu * w_e).astype(BF16)
            acc = acc + _dot(act, wd_ref[e])

        @pl.when(i >= 2)
        def _():
            wait_scatter(slot)

        ys_ref[slot] = acc
        if last:
            start_scatter(dst_ref, slot)
            wait_scatter(1 - slot)
            wait_scatter(slot)

    @pl.when(i == 0)
    def _():
        step(True, False)

    @pl.when((i > 0) & (i < last_step))
    def _():
        step(False, False)

    @pl.when(i == last_step)
    def _():
        step(False, True)


def _moe_plan(tx, counts, tm):
    t = tx.shape[0]
    group = tx[:, D_MODEL + META_GROUP_LANE].astype(jnp.int32)
    count = counts[0, :MOE_GROUPS].astype(jnp.int32)
    padded = (count + tm - 1) // tm * tm
    ends = jnp.cumsum(padded)
    starts = ends - padded
    first_of_group = jnp.cumsum(count) - count
    tiles = t // tm + MOE_GROUPS
    tile_start = jnp.arange(tiles, dtype=jnp.int32) * tm
    tile_group = jnp.sum((tile_start[:, None] >= ends[None, :]).astype(jnp.int32), axis=1)
    tile_group = jnp.minimum(tile_group, MOE_GROUPS - 1)
    order = jnp.argsort(group, stable=True).astype(jnp.int32)
    row_group = jnp.repeat(tile_group, tm)
    k = jnp.arange(tiles * tm, dtype=jnp.int32) - starts[row_group]
    is_real = k < count[row_group]
    src = jnp.where(is_real, order[jnp.clip(first_of_group[row_group] + k, 0, t - 1)], 0)
    spare = t - 1 + jnp.cumsum((~is_real).astype(jnp.int32))
    dst = jnp.where(is_real, src, spare)
    return tile_group, src.reshape(tiles, 1, tm), dst.reshape(tiles, 1, tm)


def _experts(tx, tile_group, src, dst, w_gate, w_up, w_down, layer):
    tiles, _, tm = src.shape
    assert tiles >= 3
    out_rows = tiles * tm
    idx_spec = lambda shift: pl.BlockSpec(
        (1, 1, tm), lambda i, tg: (jnp.clip(i + shift, 0, tiles - 1), 0, 0),
        memory_space=pltpu.SMEM)
    w_spec = lambda a, b: pl.BlockSpec((None, None, EXPERTS_PER_GROUP, a, b),
                                       lambda i, tg: (layer, tg[i], 0, 0, 0))
    grid_spec = pltpu.PrefetchScalarGridSpec(
        num_scalar_prefetch=1,
        grid=(tiles,),
        in_specs=[idx_spec(0), idx_spec(1), idx_spec(0), idx_spec(-1),
                  pl.BlockSpec(memory_space=pl.ANY),
                  w_spec(D_MODEL, D_EXPERT), w_spec(D_MODEL, D_EXPERT), w_spec(D_EXPERT, D_MODEL)],
        out_specs=pl.BlockSpec(memory_space=pl.ANY),
        scratch_shapes=[pltpu.VMEM((2, tm, D_MODEL + LANES), F32),
                        pltpu.VMEM((2, tm, D_MODEL), F32),
                        pltpu.SemaphoreType.DMA((2,)), pltpu.SemaphoreType.DMA((2,))],
    )
    return pl.pallas_call(
        functools.partial(_experts_kernel, tm=tm),
        grid_spec=grid_spec,
        out_shape=jax.ShapeDtypeStruct((out_rows, D_MODEL), F32),
        compiler_params=_cparams(("arbitrary",)),
        name="experts",
    )(tile_group, src, src, dst, dst, tx, w_gate, w_up, w_down)


def _ple_kernel(x_ref, moe_ref, pp_ref, ps_ref, g_ref, wg_ref, wp_ref, gf_ref, *out_refs,
                prompt_tiles, final):
    i = pl.program_id(0)
    x = x_ref[...] + moe_ref[...]
    h = _rmsnorm(x, g_ref[...]).astype(BF16)
    gate = _sigmoid(_dot(h, wg_ref[...]))
    p = jnp.where(i < prompt_tiles, pp_ref[...], ps_ref[...])
    out = x + gate * _dot(p.astype(BF16), wp_ref[...])
    results = [out, _rmsnorm(out, gf_ref[...])] if final else [out]

    @pl.when(i < prompt_tiles)
    def _():
        for n, val in enumerate(results):
            out_refs[2 * n][...] = val

    @pl.when(i == prompt_tiles)
    def _():
        for n, val in enumerate(results):
            out_refs[2 * n + 1][...] = val


def _ple(x, moe, p_prompt, p_sample, g, w_gate, w_proj, g_final, layer, final):
    t = x.shape[0]
    tm = TOKEN_TILE
    tp, ts = p_prompt.shape[1], p_sample.shape[1]
    prompt_tiles = tp // tm
    assert ts == tm and prompt_tiles + 1 == t // tm
    row = lambda width: pl.BlockSpec((tm, width), lambda i: (i, 0))
    par = lambda shape: pl.BlockSpec(shape, lambda i: (0,) * len(shape))
    p_idx = lambda i: jnp.minimum(i, prompt_tiles - 1)
    n_res = 2 if final else 1
    out_specs = [pl.BlockSpec((tm, D_MODEL), lambda i: (p_idx(i), 0)),
                 pl.BlockSpec((tm, D_MODEL), lambda i: (0, 0))] * n_res
    out_shape = [jax.ShapeDtypeStruct((tp, D_MODEL), F32),
                 jax.ShapeDtypeStruct((ts, D_MODEL), F32)] * n_res
    return pl.pallas_call(
        functools.partial(_ple_kernel, prompt_tiles=prompt_tiles, final=final),
        grid=(t // tm,),
        in_specs=[row(D_MODEL), row(D_MODEL),
                  pl.BlockSpec((None, tm, PLE_DIM), lambda i: (layer, p_idx(i), 0)),
                  pl.BlockSpec((None, tm, PLE_DIM), lambda i: (layer, 0, 0)),
                  par((1, D_MODEL)), par((D_MODEL, D_MODEL)), par((PLE_DIM, D_MODEL)),
                  par((1, D_MODEL))],
        out_specs=out_specs,
        out_shape=out_shape,
        compiler_params=_cparams(("arbitrary",)),
        name="ple",
    )(x, moe, p_prompt, p_sample, g, w_gate, w_proj, g_final)


def _pad_rows(a, rows):
    return jnp.pad(a, ((0, 0), (rows - a.shape[1], 0), (0, 0)))


def _layer(i, x_p, x_s, p_prompt, p_sample, cache_k, cache_v, page_table, state_ssm,
           state_ssm_conv, state_cf_conv, prm, dims, final):
    bp, sp_len, bs, ss_len = dims
    tp = bp * sp_len
    ts = bs * ss_len
    row = lambda v: v.reshape(1, -1)

    w_t = jnp.transpose(prm['w_in'][i])
    dt_rows = jnp.pad(w_t[OFF_DT:OFF_GLU], ((0, LANES - SSM_HEADS), (0, 0)))
    w_main_t = jnp.concatenate([w_t[:OFF_DT], w_t[OFF_GLU:OFF_GATE], dt_rows],
                               axis=0).astype(BF16)
    w_gates_t = w_t[OFF_GATE:].astype(BF16)
    g_mix = row(prm['norm_mix'][i])

    qb_p, kt_p, vt_p, ktb_p, vtb_p, z_p, xbc_p, glu_p, dt_p = _in_proj_prompt(
        x_p, g_mix, w_main_t, bp, sp_len)
    qb_s, k_s, v_s, kb_s, vb_s, z_s, xbc_s, glu_s, dt_s = _in_proj_rows(
        x_s, g_mix, w_main_t, 0, ts)

    a_prompt = _attn_prompt(qb_p, ktb_p, vtb_p, prm['sb_bias'][i], bp, sp_len)
    seq3 = lambda a: a.reshape(bs, ss_len, MIX_WIDTH)
    a_sample = _attn_decode(seq3(qb_s), seq3(kb_s), seq3(vb_s), cache_k, cache_v, page_table,
                            prm['sb_bias'][i], i).reshape(ts, MIX_WIDTH)

    pad_lanes = lambda vec, fill: jnp.pad(vec, (0, LANES - vec.shape[0]),
                                          constant_values=fill).reshape(1, LANES)
    ssd_prm = (prm['ssm_conv_w'][i], row(prm['ssm_conv_b'][i]), pad_lanes(prm['dt_bias'][i], 0.0),
               pad_lanes(-jnp.exp(prm['a_log'][i]), 0.0),
               row(jnp.repeat(prm['d_skip'][i], SSM_HEAD_DIM)), row(prm['ssm_norm'][i]))
    zeros_conv = jnp.zeros((bp, SUBLANES, SSM_CONV_CH), F32)
    zeros_h = jnp.zeros((bp, SSM_HEADS, SSM_HEAD_DIM, SSM_STATE), F32)
    b_prompt, sconv_p, h_p = _ssd(xbc_p, dt_p, z_p, zeros_conv, zeros_h, *ssd_prm,
                                  batch=bp, seq=sp_len, valid=sp_len)
    padseq = lambda a: jnp.pad(a.reshape(bs, ss_len, -1),
                               ((0, 0), (0, SSM_CHUNK - ss_len), (0, 0))).reshape(bs * SSM_CHUNK, -1)
    b_s, sconv_s, h_s = _ssd(padseq(xbc_s), padseq(dt_s), padseq(z_s),
                             _pad_rows(state_ssm_conv[i], SUBLANES), state_ssm[i], *ssd_prm,
                             batch=bs, seq=SSM_CHUNK, valid=ss_len)
    b_sample = b_s.reshape(bs, SSM_CHUNK, MIX_WIDTH)[:, :ss_len].reshape(ts, MIX_WIDTH)

    cf_prm = (prm['cf_conv_w'][i], row(prm['cf_conv_b'][i]), row(prm['cf_ln_g'][i]),
              row(prm['cf_ln_b'][i]))
    c_prompt, cf_p = _conformer(glu_p, jnp.zeros((bp, CF_HALO, MIX_WIDTH), F32), *cf_prm,
                                batch=bp, seq=sp_len)
    c_sample, cf_s = _conformer(glu_s, _pad_rows(state_cf_conv[i], CF_HALO), *cf_prm,
                                batch=bs, seq=ss_len)

    x = _merge(x_p, x_s, (a_prompt, b_prompt, c_prompt), (a_sample, b_sample, c_sample), g_mix,
               w_gates_t, prm['w_branch'][i].astype(BF16), prm['w_out'][i].astype(BF16))

    w_router = jnp.concatenate(
        [prm['w_router_expert'][i].reshape(D_MODEL, N_EXPERTS), prm['w_router_group'][i],
         jnp.zeros((D_MODEL, LANES - N_EXPERTS - MOE_GROUPS), F32)], axis=1)
    b_router = jnp.concatenate(
        [prm['b_router_expert'][i].reshape(N_EXPERTS), prm['b_router_group'][i],
         jnp.zeros((LANES - N_EXPERTS - MOE_GROUPS,), F32)]).reshape(1, LANES)
    tx, counts = _router(x, row(prm['norm_ffn'][i]), w_router, b_router)
    tile_group, src, dst = _moe_plan(tx, counts, MOE_ROW_TILE)
    moe = _experts(tx, tile_group, src, dst, prm['w_expert_gate_g'], prm['w_expert_up_g'],
                   prm['w_expert_down_g'], layer=i)

    outs = _ple(x, moe, p_prompt, p_sample, row(prm['norm_ple'][i]),
                prm['w_ple_gate'][i].astype(BF16), prm['w_ple_proj'][i].astype(BF16),
                row(prm['norm_final']), layer=i, final=final)

    conv_tail = SSM_CONV - 1
    cf_tail = CF_CONV - 1
    states = dict(
        k_p=jnp.transpose(kt_p.reshape(bp, SB_HEADS, SB_HEAD_DIM, sp_len), (0, 3, 1, 2)),
        v_p=jnp.transpose(vt_p.reshape(bp, SB_HEADS, SB_HEAD_DIM, sp_len), (0, 3, 1, 2)),
        h_p=h_p, sconv_p=sconv_p[:, SUBLANES - conv_tail:], cf_p=cf_p[:, CF_HALO - cf_tail:],
        k_s=k_s.reshape(bs, ss_len, SB_HEADS, SB_HEAD_DIM),
        v_s=v_s.reshape(bs, ss_len, SB_HEADS, SB_HEAD_DIM),
        h_s=h_s, sconv_s=sconv_s[:, SUBLANES - conv_tail:], cf_s=cf_s[:, CF_HALO - cf_tail:])
    return outs, states


def kernel(x_prompt, x_sample, p_prompt, p_sample, cache_k, cache_v, page_table, state_ssm, state_ssm_conv, state_cf_conv, norm_mix, w_in, sb_bias, ssm_conv_w, ssm_conv_b, dt_bias, a_log, d_skip, ssm_norm, cf_conv_w, cf_conv_b, cf_ln_g, cf_ln_b, w_branch, w_out, norm_ffn, w_router_group, b_router_group, w_router_expert, b_router_expert, w_expert_gate, w_expert_up, w_expert_down, norm_ple, w_ple_gate, w_ple_proj, norm_final):
    prm = dict(norm_mix=norm_mix, w_in=w_in, sb_bias=sb_bias, ssm_conv_w=ssm_conv_w,
               ssm_conv_b=ssm_conv_b, dt_bias=dt_bias, a_log=a_log, d_skip=d_skip,
               ssm_norm=ssm_norm, cf_conv_w=cf_conv_w, cf_conv_b=cf_conv_b, cf_ln_g=cf_ln_g,
               cf_ln_b=cf_ln_b, w_branch=w_branch, w_out=w_out, norm_ffn=norm_ffn,
               w_router_group=w_router_group, b_router_group=b_router_group,
               w_router_expert=w_router_expert, b_router_expert=b_router_expert,
               w_expert_gate=w_expert_gate, w_expert_up=w_expert_up,
               w_expert_down=w_expert_down, norm_ple=norm_ple, w_ple_gate=w_ple_gate,
               w_ple_proj=w_ple_proj, norm_final=norm_final)
    depth = w_in.shape[0]
    for name in ('w_expert_gate', 'w_expert_up', 'w_expert_down'):
        w = prm[name]
        prm[name + '_g'] = w.astype(BF16).reshape(
            (depth, MOE_GROUPS, EXPERTS_PER_GROUP) + w.shape[2:])
    bp, sp_len, _ = x_prompt.shape
    bs, ss_len, _ = x_sample.shape
    tp = bp * sp_len
    ts = bs * ss_len
    dims = (bp, sp_len, bs, ss_len)
    n_pool = cache_k.shape[1]
    ck = jnp.transpose(cache_k, (0, 1, 3, 4, 2)).reshape(depth, n_pool, MIX_WIDTH, PAGE_SIZE)
    cv = jnp.transpose(cache_v, (0, 1, 3, 4, 2)).reshape(depth, n_pool, MIX_WIDTH, PAGE_SIZE)

    x_p = x_prompt.reshape(tp, D_MODEL)
    x_s = x_sample.reshape(ts, D_MODEL)
    pp = p_prompt.reshape(depth, tp, PLE_DIM)
    ps = p_sample.reshape(depth, ts, PLE_DIM)
    per_layer = []
    for i in range(depth):
        outs, st = _layer(i, x_p, x_s, pp, ps, ck, cv, page_table, state_ssm, state_ssm_conv,
                          state_cf_conv, prm, dims, final=(i == depth - 1))
        x_p, x_s = outs[0], outs[1]
        per_layer.append(st)
    y_p, y_s = outs[2], outs[3]
    stack = lambda name: jnp.stack([st[name] for st in per_layer])
    return (y_p.reshape(bp, sp_len, D_MODEL), y_s.reshape(bs, ss_len, D_MODEL),
            stack('k_p'), stack('v_p'), stack('h_p'), stack('sconv_p'), stack('cf_p'),
            stack('k_s'), stack('v_s'), stack('h_s'), stack('sconv_s'), stack('cf_s'))
```

```python
import functools

import jax
import jax.numpy as jnp
from jax import lax
from jax.experimental import pallas as pl
from jax.experimental.pallas import tpu as pltpu

F32 = jnp.float32
BF16 = jnp.bfloat16
EPS = 1e-6

D_MODEL = 1024
MIX_WIDTH = 512
N_BRANCH = 3
SB_HEADS = 8
SB_HEAD_DIM = 64
SSM_HEADS = 8
SSM_HEAD_DIM = 64
SSM_GROUPS = 2
SSM_STATE = 128
SSM_CONV = 4
SSM_CONV_CH = MIX_WIDTH + 2 * SSM_GROUPS * SSM_STATE
SSM_CHUNK = 128
CF_CONV = 31
MOE_GROUPS = 4
EXPERTS_PER_GROUP = 4
N_EXPERTS = MOE_GROUPS * EXPERTS_PER_GROUP
D_EXPERT = 512
PLE_DIM = 256
PAGE_SIZE = 128
OFF_DT = 3 * MIX_WIDTH + MIX_WIDTH + SSM_CONV_CH
OFF_GLU = OFF_DT + SSM_HEADS
OFF_GATE = OFF_GLU + 2 * MIX_WIDTH

LANES = 128
SUBLANES = 8
VMEM_LIMIT = 56 * 1024 * 1024

TOKEN_TILE = 256
MOE_ROW_TILE = 256
ROUTER_TILE = 640
ATTN_TILE = 256
ATTN_GROUP = 4
CF_TILE = 256
DEC_PAGES_PER_STEP = 16


def _cparams(sem):
    return pltpu.CompilerParams(dimension_semantics=sem, vmem_limit_bytes=VMEM_LIMIT)


def _rmsnorm(x, g):
    return x * lax.rsqrt(jnp.mean(x * x, axis=-1, keepdims=True) + EPS) * g


def _sigmoid(x):
    return 1.0 / (1.0 + jnp.exp(-x))


def _silu(x):
    return x * _sigmoid(x)


def _softplus(x):
    return jnp.maximum(x, 0.0) + jnp.log(1.0 + jnp.exp(-jnp.abs(x)))


LOG2E = 1.4426950408889634


def _softplus2(x):
    return jnp.maximum(x, 0.0) + jnp.log2(1.0 + jnp.exp2(-jnp.abs(x)))


def _dot(a, b):
    return jnp.dot(a, b, preferred_element_type=F32)


def _dot_nt(a, b):
    return lax.dot_general(a, b, (((1,), (1,)), ((), ())), preferred_element_type=F32)


def _dot_tn(a, b):
    return lax.dot_general(a, b, (((0,), (0,)), ((), ())), preferred_element_type=F32)


def _split3(x):
    hi = x.astype(BF16)
    r = x - hi.astype(F32)
    mid = r.astype(BF16)
    lo = (r - mid.astype(F32)).astype(BF16)
    return hi, mid, lo


def _dot_exact_lhs(mat_bf16, x):
    hi, mid, lo = _split3(x)
    return _dot(mat_bf16, hi) + _dot(mat_bf16, mid) + _dot(mat_bf16, lo)


IN_PROJ_WIDTHS = [MIX_WIDTH, MIX_WIDTH, MIX_WIDTH, MIX_WIDTH, MIX_WIDTH, MIX_WIDTH,
                  SSM_CONV_CH, 2 * MIX_WIDTH, LANES]
IN_PROJ_DTYPES = [BF16, F32, F32, BF16, BF16, F32, F32, F32, F32]


def _in_proj_kernel(x_ref, g_ref, w_ref, q_ref, k_ref, v_ref, kb_ref, vb_ref,
                    z_ref, xbc_ref, glu_ref, dt_ref, *, transposed_kv):
    h = _rmsnorm(x_ref[...], g_ref[...]).astype(BF16)
    w = MIX_WIDTH
    if transposed_kv:
        q = _dot_nt(h, w_ref[0:w, :])
        kv = _dot_nt(w_ref[w:3 * w, :], h)
        k = kv[0:w]
        v = kv[w:2 * w]
        rest = _dot_nt(h, w_ref[3 * w:, :])
    else:
        p = _dot_nt(h, w_ref[...])
        q = p[:, 0:w]
        k = p[:, w:2 * w]
        v = p[:, 2 * w:3 * w]
        rest = p[:, 3 * w:]
    q_ref[...] = (q * (SB_HEAD_DIM ** -0.5 * LOG2E)).astype(BF16)
    k_ref[...] = k
    v_ref[...] = v
    kb_ref[...] = k.astype(BF16)
    vb_ref[...] = v.astype(BF16)
    z_ref[...] = rest[:, 0:w]
    xbc_ref[...] = rest[:, w:3 * w]
    glu_ref[...] = rest[:, 3 * w:5 * w]
    dt_ref[...] = rest[:, 5 * w:5 * w + LANES]


def _in_proj_prompt(x, g, w_main_t, batch, seq):
    tm = TOKEN_TILE
    nt = seq // tm
    n = w_main_t.shape[0]
    row = lambda width: pl.BlockSpec((tm, width), lambda b, j: (b * nt + j, 0))
    par = lambda shape: pl.BlockSpec(shape, lambda b, j: (0,) * len(shape))
    out_specs = [row(wd) for wd in IN_PROJ_WIDTHS]
    out_shape = [jax.ShapeDtypeStruct((batch * seq, wd), dt)
                 for wd, dt in zip(IN_PROJ_WIDTHS, IN_PROJ_DTYPES)]
    for idx in (1, 2):
        out_specs[idx] = pl.BlockSpec((None, MIX_WIDTH, tm), lambda b, j: (b, 0, j))
        out_shape[idx] = jax.ShapeDtypeStruct((batch, MIX_WIDTH, seq), F32)
    for idx in (3, 4):
        out_specs[idx] = pl.BlockSpec((None, None, MIX_WIDTH, tm), lambda b, j: (b, j, 0, 0))
        out_shape[idx] = jax.ShapeDtypeStruct((batch, nt, MIX_WIDTH, tm), BF16)
    return pl.pallas_call(
        functools.partial(_in_proj_kernel, transposed_kv=True),
        grid=(batch, nt),
        in_specs=[row(D_MODEL), par((1, D_MODEL)), par((n, D_MODEL))],
        out_specs=out_specs,
        out_shape=out_shape,
        compiler_params=_cparams(("parallel", "parallel")),
        name="in_proj_prompt",
    )(x, g, w_main_t)


def _in_proj_rows(x, g, w_main_t, row_start, rows):
    tm = TOKEN_TILE
    first = row_start // tm
    n = w_main_t.shape[0]
    par = lambda shape: pl.BlockSpec(shape, lambda i: (0,) * len(shape))
    return pl.pallas_call(
        functools.partial(_in_proj_kernel, transposed_kv=False),
        grid=(rows // tm,),
        in_specs=[pl.BlockSpec((tm, D_MODEL), lambda i: (i + first, 0)),
                  par((1, D_MODEL)), par((n, D_MODEL))],
        out_specs=[pl.BlockSpec((tm, wd), lambda i: (i, 0)) for wd in IN_PROJ_WIDTHS],
        out_shape=[jax.ShapeDtypeStruct((rows, wd), dt)
                   for wd, dt in zip(IN_PROJ_WIDTHS, IN_PROJ_DTYPES)],
        compiler_params=_cparams(("parallel",)),
        name="in_proj_rows",
    )(x, g, w_main_t)


def _strict_upper(n):
    r = lax.broadcasted_iota(jnp.int32, (n, n), 0)
    c = lax.broadcasted_iota(jnp.int32, (n, n), 1)
    return jnp.where(r > c, 1.0, 0.0).astype(BF16)


def _sb_block(z, v_blk, upper, carry, mask, v_channel_major=False):
    sp = _softplus2(z)
    if mask is not None:
        sp = jnp.where(mask, sp, 0.0)
    after = _dot(sp.astype(BF16), upper)
    w = jnp.exp2(z - sp - after - carry)
    if mask is not None:
        w = jnp.where(mask, w, 0.0)
    pv = _dot_nt if v_channel_major else _dot
    return pv(w.astype(BF16), v_blk), jnp.sum(sp, axis=1, keepdims=True)


def _attn_prompt_kernel(bias_ref, q_ref, k_ref, v_ref, o_ref, acc_ref, carry_ref, *, seq, tile):
    hp = pl.program_id(1)
    n_q = seq // tile
    upper = _strict_upper(tile)
    lane = lax.broadcasted_iota(jnp.int32, (tile, LANES), 1)
    head0 = lane < SB_HEAD_DIM
    r = lax.broadcasted_iota(jnp.int32, (tile, tile), 0)
    c = lax.broadcasted_iota(jnp.int32, (tile, tile), 1)
    diag_mask = jnp.concatenate([c < r, c < r], axis=0)
    bias0 = bias_ref[hp * 2] * LOG2E
    bias1 = bias_ref[hp * 2 + 1] * LOG2E

    def visit(pairs, mask):
        qis = [qi for qi, _ in pairs]
        kjs = [kj for _, kj in pairs]
        zs, sps, afters, carries = [], [], [], []
        for qi, kj in pairs:
            q = q_ref[qi * tile:(qi + 1) * tile, :]
            zero = jnp.zeros_like(q)
            q2 = jnp.concatenate([jnp.where(head0, q, zero), jnp.where(head0, zero, q)], axis=0)
            d = _dot(q2, k_ref[kj])
            zs.append(jnp.concatenate([d[:tile] + bias0, d[tile:] + bias1], axis=0))
        for z in zs:
            sp = _softplus2(z)
            sps.append(sp if mask is None else jnp.where(mask, sp, 0.0))
        for sp in sps:
            afters.append(_dot(sp.astype(BF16), upper))
        contribs = []
        for qi, kj, z, sp, after in zip(qis, kjs, zs, sps, afters):
            if mask is None:
                carry = carry_ref[qi]
                carries.append(carry)
                after = after + jnp.concatenate([carry] * (tile // LANES), axis=1)
            w = jnp.exp2(z - sp - after)
            if mask is not None:
                w = jnp.where(mask, w, 0.0)
            contribs.append(_dot_nt(w.astype(BF16), v_ref[kj]))
        for n, (qi, sp, contrib) in enumerate(zip(qis, sps, contribs)):
            row_sum = jnp.broadcast_to(jnp.sum(sp, axis=1, keepdims=True), (2 * tile, LANES))
            if mask is None:
                acc_ref[qi] += contrib
                carry_ref[qi] = carries[n] + row_sum
            else:
                acc_ref[qi] = contrib
                carry_ref[qi] = row_sum

    for d in range(n_q):
        wave = [(qi, qi - d) for qi in range(d, n_q)]
        n_groups = -(-len(wave) // ATTN_GROUP)
        size = -(-len(wave) // n_groups)
        for g in range(n_groups):
            visit(wave[g * size:(g + 1) * size], diag_mask if d == 0 else None)

    for qi in range(n_q):
        acc = acc_ref[qi]
        o_ref[qi * tile:(qi + 1) * tile, :] = jnp.where(head0, acc[:tile], acc[tile:])


def _attn_prompt(q, kt, vt, sb_bias, batch, seq):
    tile = kt.shape[-1]
    nt = seq // tile
    blk = pl.BlockSpec((seq, LANES), lambda b, hp: (b, hp))
    kv_blk = pl.BlockSpec((None, nt, LANES, tile), lambda b, hp: (b, 0, hp, 0))
    return pl.pallas_call(
        functools.partial(_attn_prompt_kernel, seq=seq, tile=tile),
        grid=(batch, SB_HEADS // 2),
        in_specs=[pl.BlockSpec(memory_space=pltpu.SMEM), blk, kv_blk, kv_blk],
        out_specs=blk,
        out_shape=jax.ShapeDtypeStruct((batch * seq, MIX_WIDTH), F32),
        scratch_shapes=[pltpu.VMEM((nt, 2 * tile, LANES), F32),
                        pltpu.VMEM((nt, 2 * tile, LANES), F32)],
        compiler_params=_cparams(("parallel", "parallel")),
        name="attn_prompt",
    )(sb_bias, q, kt, vt)


def _attn_decode_kernel(pt_ref, bias_ref, q_ref, kn_ref, vn_ref, *rest, pages, dec_seq):
    k_refs = rest[:pages]
    v_refs = rest[pages:2 * pages]
    o_ref = rest[2 * pages]
    acc_ref, carry_ref = rest[2 * pages + 1:]
    step = pl.program_id(1)
    rows = SB_HEADS * dec_seq
    width = MIX_WIDTH
    upper = _strict_upper(PAGE_SIZE)
    r_w = lax.broadcasted_iota(jnp.int32, (rows, width), 0)
    c_w = lax.broadcasted_iota(jnp.int32, (rows, width), 1)
    head_mask = (r_w // dec_seq) == (c_w // SB_HEAD_DIM)
    q_rep = jnp.concatenate([q_ref[0].astype(F32)] * SB_HEADS, axis=0)
    q_bd = jnp.where(head_mask, q_rep, 0.0).astype(BF16)
    r_k = lax.broadcasted_iota(jnp.int32, (rows, PAGE_SIZE), 0)
    c_k = lax.broadcasted_iota(jnp.int32, (rows, PAGE_SIZE), 1)
    bias = jnp.zeros((rows, PAGE_SIZE), F32)
    for h in range(SB_HEADS):
        bias = jnp.where(r_k // dec_seq == h, bias_ref[h] * LOG2E, bias)

    @pl.when(step == 0)
    def _():
        pad = jnp.zeros((PAGE_SIZE - dec_seq, width), F32)
        kn = jnp.concatenate([kn_ref[0].astype(F32), pad], axis=0).astype(BF16)
        vn = jnp.concatenate([vn_ref[0].astype(F32), pad], axis=0).astype(BF16)
        z = _dot_nt(q_bd, kn) + bias
        mask = c_k < (r_k % dec_seq)
        contrib, rs = _sb_block(z, vn, upper, 0.0, mask)
        acc_ref[...] = contrib
        carry_ref[...] = jnp.broadcast_to(rs, (rows, PAGE_SIZE))

    k_all = jnp.concatenate([k_refs[j][...].astype(BF16) for j in range(pages)], axis=1)
    z = _dot(q_bd, k_all)
    z = jnp.concatenate([z[:, j * PAGE_SIZE:(j + 1) * PAGE_SIZE] + bias for j in range(pages)],
                        axis=0)
    sp = _softplus2(z)
    after = _dot(sp.astype(BF16), upper)
    row_sum = jnp.sum(sp, axis=1, keepdims=True)
    carry = carry_ref[...]
    carries = []
    for j in range(pages):
        carries.append(carry)
        carry = carry + row_sum[j * rows:(j + 1) * rows]
    w = jnp.exp2(z - sp - after - jnp.concatenate(carries, axis=0)).astype(BF16)
    w_all = jnp.concatenate([w[j * rows:(j + 1) * rows] for j in range(pages)], axis=1)
    v_all = jnp.concatenate([v_refs[j][...].astype(BF16) for j in range(pages)], axis=1)
    acc = acc_ref[...] + _dot_nt(w_all, v_all)
    acc_ref[...] = acc
    carry_ref[...] = carry

    @pl.when(step == pl.num_programs(1) - 1)
    def _():
        a = jnp.where(head_mask, acc, 0.0)
        out = a[0:dec_seq]
        for h in range(1, SB_HEADS):
            out = out + a[h * dec_seq:(h + 1) * dec_seq]
        o_ref[0] = out


def _attn_decode(q, k_new, v_new, cache_k, cache_v, page_table, sb_bias, layer):
    bsz, dec_seq, width = q.shape
    n_pages = page_table.shape[1]
    pages = DEC_PAGES_PER_STEP
    assert n_pages % pages == 0 and dec_seq == SUBLANES
    steps = n_pages // pages
    rows = SB_HEADS * dec_seq

    def page_spec(j):
        def imap(b, s, pt):
            return (layer, pt[b * n_pages + (n_pages - 1 - (s * pages + j))], 0, 0)
        return pl.BlockSpec((None, None, width, PAGE_SIZE), imap)

    seq_spec = pl.BlockSpec((1, dec_seq, width), lambda b, s, pt: (b, 0, 0))
    grid_spec = pltpu.PrefetchScalarGridSpec(
        num_scalar_prefetch=1,
        grid=(bsz, steps),
        in_specs=[pl.BlockSpec(memory_space=pltpu.SMEM), seq_spec, seq_spec, seq_spec]
        + [page_spec(j) for j in range(pages)] * 2,
        out_specs=seq_spec,
        scratch_shapes=[pltpu.VMEM((rows, width), F32), pltpu.VMEM((rows, PAGE_SIZE), F32)],
    )
    return pl.pallas_call(
        functools.partial(_attn_decode_kernel, pages=pages, dec_seq=dec_seq),
        grid_spec=grid_spec,
        out_shape=jax.ShapeDtypeStruct((bsz, dec_seq, width), F32),
        compiler_params=_cparams(("parallel", "arbitrary")),
        name="attn_decode",
    )(page_table.reshape(-1), sb_bias, q, k_new, v_new,
      *([cache_k] * pages), *([cache_v] * pages))


def _expand_heads(cols, width):
    src = lax.broadcasted_iota(jnp.int32, (LANES, width), 0)
    dst = lax.broadcasted_iota(jnp.int32, (LANES, width), 1)
    onehot = jnp.where(src == dst // SSM_HEAD_DIM, 1.0, 0.0).astype(BF16)
    hi, mid, lo = _split3(cols)
    return _dot(hi, onehot) + _dot(mid, onehot) + _dot(lo, onehot)


def _ssd_kernel(xbc_ref, dt_ref, z_ref, buf_ref, h0_ref, cw_ref, cb_ref, dtb_ref, a_ref,
                dskip_ref, ng_ref, y_ref, conv_ref, h_ref, xp_ref, hs_ref, *, chunk, valid):
    c = pl.program_id(1)
    q = chunk
    halo = SUBLANES

    @pl.when(c == 0)
    def _():
        xp_ref[0:halo, :] = buf_ref[0]
        hs_ref[...] = h0_ref[0]

    xp_ref[halo:halo + q, :] = xbc_ref[...]
    acc = cb_ref[...] + cw_ref[SSM_CONV - 1:SSM_CONV, :] * xp_ref[halo:halo + q, :]
    for j in range(SSM_CONV - 1):
        off = halo - (SSM_CONV - 1) + j
        acc = acc + cw_ref[j:j + 1, :] * xp_ref[off:off + q, :]
    last = min(valid, q)
    tail = xp_ref[last:last + halo, :]
    conv_ref[0] = tail
    xp_ref[0:halo, :] = tail
    xc = _silu(acc)
    xs = xc[:, 0:MIX_WIDTH]
    gw = SSM_GROUPS * SSM_STATE
    bm = xc[:, MIX_WIDTH:MIX_WIDTH + gw]
    cm = xc[:, MIX_WIDTH + gw:MIX_WIDTH + 2 * gw]

    dt = _softplus(dt_ref[...] + dtb_ref[...])
    if valid < q:
        row = lax.broadcasted_iota(jnp.int32, (q, LANES), 0)
        dt = jnp.where(row < valid, dt, 0.0)
    da = dt * a_ref[...]
    r = lax.broadcasted_iota(jnp.int32, (q, q), 0)
    s = lax.broadcasted_iota(jnp.int32, (q, q), 1)
    causal = s <= r
    tril = jnp.where(causal, 1.0, 0.0).astype(BF16)
    cum = _dot_exact_lhs(tril, da)
    cum_t = jnp.transpose(cum)
    dt_t = jnp.transpose(dt)
    cum_last = cum[q - 1:q, :]
    w_end = jnp.exp(cum_last - cum) * dt
    e_cum = jnp.exp(cum)

    hpg = SSM_HEADS // SSM_GROUPS
    gwid = hpg * SSM_HEAD_DIM
    lane_g = lax.broadcasted_iota(jnp.int32, (q, gwid), 1)
    xs_b = xs.astype(BF16)
    e_cum_x = _expand_heads(e_cum, MIX_WIDTH)
    w_end_x = _expand_heads(w_end, MIX_WIDTH)
    xw = (xs * w_end_x).astype(BF16)
    y_parts = []
    for g in range(SSM_GROUPS):
        b_g = bm[:, g * SSM_STATE:(g + 1) * SSM_STATE].astype(BF16)
        c_g = cm[:, g * SSM_STATE:(g + 1) * SSM_STATE].astype(BF16)
        cb = _dot_nt(c_g, b_g)
        xs_g = xs_b[:, g * gwid:(g + 1) * gwid]
        y_g = jnp.zeros((q, gwid), F32)
        for e in range(hpg):
            hd = g * hpg + e
            seg = cum[:, hd:hd + 1] - cum_t[hd:hd + 1, :]
            decay = jnp.where(causal, jnp.exp(jnp.where(causal, seg, 0.0)), 0.0)
            w = (cb * decay * dt_t[hd:hd + 1, :]).astype(BF16)
            y_g = jnp.where(lane_g // SSM_HEAD_DIM == e, _dot(w, xs_g), y_g)
        h_g = hs_ref[g * hpg:(g + 1) * hpg].reshape(gwid, SSM_STATE)
        y_prev = _dot_nt(c_g, h_g.astype(BF16))
        y_parts.append(y_g + y_prev * e_cum_x[:, g * gwid:(g + 1) * gwid])
        upd = _dot_tn(xw[:, g * gwid:(g + 1) * gwid], b_g)
        for e in range(hpg):
            hd = g * hpg + e
            scale = jnp.exp(cum_last[:, hd:hd + 1])
            hs_ref[hd] = hs_ref[hd] * scale + upd[e * SSM_HEAD_DIM:(e + 1) * SSM_HEAD_DIM, :]
    y = jnp.concatenate(y_parts, axis=1) + dskip_ref[...] * xs
    y = y * _silu(z_ref[...])
    y_ref[...] = _rmsnorm(y, ng_ref[...])
    h_ref[0] = hs_ref[...]


def _ssd(xbc, dt_raw, z, conv_buf, h0, conv_w, conv_b, dt_bias, a_neg, d_skip_row, norm_g,
         batch, seq, valid):
    q = SSM_CHUNK
    nc = seq // q
    row = lambda width: pl.BlockSpec((q, width), lambda b, c: (b * nc + c, 0))
    par = lambda shape: pl.BlockSpec(shape, lambda b, c: (0,) * len(shape))
    return pl.pallas_call(
        functools.partial(_ssd_kernel, chunk=q, valid=valid),
        grid=(batch, nc),
        in_specs=[row(SSM_CONV_CH), row(LANES), row(MIX_WIDTH),
                  pl.BlockSpec((1, SUBLANES, SSM_CONV_CH), lambda b, c: (b, 0, 0)),
                  pl.BlockSpec((1, SSM_HEADS, SSM_HEAD_DIM, SSM_STATE), lambda b, c: (b, 0, 0, 0)),
                  par((SSM_CONV, SSM_CONV_CH)), par((1, SSM_CONV_CH)), par((1, LANES)),
                  par((1, LANES)), par((1, MIX_WIDTH)), par((1, MIX_WIDTH))],
        out_specs=[row(MIX_WIDTH),
                   pl.BlockSpec((1, SUBLANES, SSM_CONV_CH), lambda b, c: (b, 0, 0)),
                   pl.BlockSpec((1, SSM_HEADS, SSM_HEAD_DIM, SSM_STATE), lambda b, c: (b, 0, 0, 0))],
        out_shape=[jax.ShapeDtypeStruct((batch * seq, MIX_WIDTH), F32),
                   jax.ShapeDtypeStruct((batch, SUBLANES, SSM_CONV_CH), F32),
                   jax.ShapeDtypeStruct((batch, SSM_HEADS, SSM_HEAD_DIM, SSM_STATE), F32)],
        scratch_shapes=[pltpu.VMEM((q + 2 * SUBLANES, SSM_CONV_CH), F32),
                        pltpu.VMEM((SSM_HEADS, SSM_HEAD_DIM, SSM_STATE), F32)],
        compiler_params=_cparams(("parallel", "arbitrary")),
        name="ssd",
    )(xbc, dt_raw, z, conv_buf, h0, conv_w, conv_b, dt_bias, a_neg, d_skip_row, norm_g)


CF_HALO = 32


def _conformer_kernel(glu_ref, buf_ref, cw_ref, cb_ref, lg_ref, lb_ref, y_ref, new_ref,
                      xp_ref, *, tile):
    c = pl.program_id(1)

    @pl.when(c == 0)
    def _():
        xp_ref[0:CF_HALO, :] = buf_ref[0]
        xp_ref[CF_HALO + tile:, :] = jnp.zeros((SUBLANES, MIX_WIDTH), F32)

    glu = glu_ref[...]
    u = glu[:, 0:MIX_WIDTH] * _sigmoid(glu[:, MIX_WIDTH:2 * MIX_WIDTH])
    xp_ref[CF_HALO:CF_HALO + tile, :] = u
    first = CF_HALO - (CF_CONV - 1)
    acc = cb_ref[...]
    for b in range(SUBLANES):
        part = None
        for a in range((first + CF_CONV - 1) // SUBLANES + 1):
            j = a * SUBLANES + b - first
            if 0 <= j < CF_CONV:
                term = cw_ref[j:j + 1, :] * xp_ref[a * SUBLANES:a * SUBLANES + tile + SUBLANES, :]
                part = term if part is None else part + term
        acc = acc + part[b:b + tile]
    tail = xp_ref[tile:tile + CF_HALO, :]
    new_ref[0] = tail
    xp_ref[0:CF_HALO, :] = tail
    mu = jnp.mean(acc, axis=-1, keepdims=True)
    xc = acc - mu
    var = jnp.mean(xc * xc, axis=-1, keepdims=True)
    y_ref[...] = _silu(xc * lax.rsqrt(var + EPS) * lg_ref[...] + lb_ref[...])


def _conformer(glu, conv_buf, conv_w, conv_b, ln_g, ln_b, batch, seq):
    tile = CF_TILE if seq % CF_TILE == 0 else seq
    nt = seq // tile
    par = lambda shape: pl.BlockSpec(shape, lambda b, c: (0,) * len(shape))
    buf_spec = pl.BlockSpec((1, CF_HALO, MIX_WIDTH), lambda b, c: (b, 0, 0))
    return pl.pallas_call(
        functools.partial(_conformer_kernel, tile=tile),
        grid=(batch, nt),
        in_specs=[pl.BlockSpec((tile, 2 * MIX_WIDTH), lambda b, c: (b * nt + c, 0)), buf_spec,
                  par((CF_CONV, MIX_WIDTH)), par((1, MIX_WIDTH)), par((1, MIX_WIDTH)),
                  par((1, MIX_WIDTH))],
        out_specs=[pl.BlockSpec((tile, MIX_WIDTH), lambda b, c: (b * nt + c, 0)), buf_spec],
        out_shape=[jax.ShapeDtypeStruct((batch * seq, MIX_WIDTH), F32),
                   jax.ShapeDtypeStruct((batch, CF_HALO, MIX_WIDTH), F32)],
        scratch_shapes=[pltpu.VMEM((tile + CF_HALO + SUBLANES, MIX_WIDTH), F32)],
        compiler_params=_cparams(("parallel", "arbitrary")),
        name="conformer",
    )(glu, conv_buf, conv_w, conv_b, ln_g, ln_b)


def _merge_kernel(xp_ref, xs_ref, ap_ref, bp_ref, cp_ref, as_ref, bs_ref, cs_ref, g_ref, wg_ref,
                  wb_ref, wo_ref, o_ref, *, prompt_tiles):
    is_prompt = pl.program_id(0) < prompt_tiles
    x = jnp.where(is_prompt, xp_ref[...], xs_ref[...])
    h = _rmsnorm(x, g_ref[...]).astype(BF16)
    merged = jnp.zeros(x.shape, F32)
    for n, (br_p, br_s) in enumerate(((ap_ref, as_ref), (bp_ref, bs_ref), (cp_ref, cs_ref))):
        br = jnp.where(is_prompt, br_p[...], br_s[...])
        gate = _sigmoid(_dot_nt(h, wg_ref[n * D_MODEL:(n + 1) * D_MODEL, :]))
        merged = merged + gate * _dot(br.astype(BF16), wb_ref[n])
    o_ref[...] = x + _dot(merged.astype(BF16), wo_ref[...])


def _merge(x_p, x_s, prompt_branches, sample_branches, g, w_gates, w_branch, w_out):
    tm = TOKEN_TILE
    prompt_tiles = x_p.shape[0] // tm
    assert x_s.shape[0] == tm and sample_branches[0].shape[0] == tm
    p_row = lambda width: pl.BlockSpec((tm, width),
                                       lambda i: (jnp.minimum(i, prompt_tiles - 1), 0))
    s_row = lambda width: pl.BlockSpec((tm, width), lambda i: (0, 0))
    par = lambda shape: pl.BlockSpec(shape, lambda i: (0,) * len(shape))
    return pl.pallas_call(
        functools.partial(_merge_kernel, prompt_tiles=prompt_tiles),
        grid=(prompt_tiles + 1,),
        in_specs=[p_row(D_MODEL), s_row(D_MODEL)] + [p_row(MIX_WIDTH)] * 3 + [s_row(MIX_WIDTH)] * 3
        + [par((1, D_MODEL)), par((N_BRANCH * D_MODEL, D_MODEL)),
           par((N_BRANCH, MIX_WIDTH, D_MODEL)), par((D_MODEL, D_MODEL))],
        out_specs=pl.BlockSpec((tm, D_MODEL), lambda i: (i, 0)),
        out_shape=jax.ShapeDtypeStruct(((prompt_tiles + 1) * tm, D_MODEL), F32),
        compiler_params=_cparams(("parallel",)),
        name="merge",
    )(x_p, x_s, *prompt_branches, *sample_branches, g, w_gates, w_branch, w_out)


META_GROUP_LANE = N_EXPERTS


def _router_kernel(x_ref, g_ref, wr_ref, br_ref, tx_ref, cnt_ref, run_ref):
    i = pl.program_id(0)

    @pl.when(i == 0)
    def _():
        run_ref[...] = jnp.zeros_like(run_ref)

    t = _rmsnorm(x_ref[...], g_ref[...])
    tx_ref[:, 0:D_MODEL] = t
    logits = jnp.dot(t, wr_ref[...], preferred_element_type=F32,
                     precision=lax.Precision.HIGHEST) + br_ref[...]
    lane = lax.broadcasted_iota(jnp.int32, logits.shape, 1)
    neg = -jnp.inf
    is_group = (lane >= N_EXPERTS) & (lane < N_EXPERTS + MOE_GROUPS)
    gl = jnp.where(is_group, logits, neg)
    gmax = jnp.max(gl, axis=1, keepdims=True)
    g_idx = jnp.min(jnp.where(gl == gmax, lane, LANES), axis=1, keepdims=True) - N_EXPERTS
    g_w = 1.0 / jnp.sum(jnp.where(is_group, jnp.exp(logits - gmax), 0.0), axis=1, keepdims=True)
    in_group = (lane // EXPERTS_PER_GROUP) == g_idx
    el = jnp.where(in_group, logits, neg)
    v1 = jnp.max(el, axis=1, keepdims=True)
    i1 = jnp.min(jnp.where(el == v1, lane, LANES), axis=1, keepdims=True)
    el2 = jnp.where(lane == i1, neg, el)
    v2 = jnp.max(el2, axis=1, keepdims=True)
    i2 = jnp.min(jnp.where(el2 == v2, lane, LANES), axis=1, keepdims=True)
    e21 = jnp.exp(v2 - v1)
    p1 = 1.0 / (1.0 + e21)
    p2 = e21 * p1
    comb = jnp.where(lane == i1, p1 * g_w, 0.0) + jnp.where(lane == i2, p2 * g_w, 0.0)
    onehot = jnp.where(lane == g_idx, 1.0, 0.0)
    run = run_ref[...] + jnp.sum(onehot, axis=0, keepdims=True)
    run_ref[...] = run
    cnt_ref[...] = run
    tx_ref[:, D_MODEL:] = jnp.where(lane == META_GROUP_LANE, g_idx.astype(F32), comb)


def _router(x, g, w_router, b_router):
    t = x.shape[0]
    tm = ROUTER_TILE if t % ROUTER_TILE == 0 else TOKEN_TILE
    row = lambda width: pl.BlockSpec((tm, width), lambda i: (i, 0))
    par = lambda shape: pl.BlockSpec(shape, lambda i: (0,) * len(shape))
    return pl.pallas_call(
        _router_kernel,
        grid=(t // tm,),
        in_specs=[row(D_MODEL), par((1, D_MODEL)), par((D_MODEL, LANES)), par((1, LANES))],
        out_specs=[row(D_MODEL + LANES), par((1, LANES))],
        out_shape=[jax.ShapeDtypeStruct((t, D_MODEL + LANES), F32),
                   jax.ShapeDtypeStruct((1, LANES), F32)],
        scratch_shapes=[pltpu.VMEM((1, LANES), F32)],
        compiler_params=_cparams(("arbitrary",)),
        name="router",
    )(x, g, w_router, b_router)


def _experts_kernel(tile_group_ref, src_ref, src_next_ref, dst_ref, dst_prev_ref, tx_hbm,
                    wg_ref, wu_ref, wd_ref, y_hbm, xs_ref, ys_ref, gsem, ssem, *, tm):
    i = pl.program_id(0)
    last_step = pl.num_programs(0) - 1
    slot = i % 2

    def start_gather(idx_ref, s):
        for r in range(tm):
            pltpu.make_async_copy(tx_hbm.at[pl.ds(idx_ref[0, 0, r], 1)],
                                  xs_ref.at[s, pl.ds(r, 1)], gsem.at[s]).start()

    def start_scatter(idx_ref, s):
        for r in range(tm):
            pltpu.make_async_copy(ys_ref.at[s, pl.ds(r, 1)],
                                  y_hbm.at[pl.ds(idx_ref[0, 0, r], 1)], ssem.at[s]).start()

    def wait_gather(s):
        pltpu.make_async_copy(tx_hbm.at[pl.ds(0, tm)], xs_ref.at[s], gsem.at[s]).wait()

    def wait_scatter(s):
        pltpu.make_async_copy(ys_ref.at[s], y_hbm.at[pl.ds(0, tm)], ssem.at[s]).wait()

    def step(first, last):
        if first:
            start_gather(src_ref, slot)
        wait_gather(slot)
        if not last:
            start_gather(src_next_ref, 1 - slot)
        if not first:
            start_scatter(dst_prev_ref, 1 - slot)
        rows = xs_ref[slot]
        x = rows[:, 0:D_MODEL].astype(BF16)
        meta = rows[:, D_MODEL:]
        lane = lax.broadcasted_iota(jnp.int32, meta.shape, 1)
        first_expert = tile_group_ref[i] * EXPERTS_PER_GROUP
        acc = jnp.zeros((tm, D_MODEL), F32)
        for e in range(EXPERTS_PER_GROUP):
            w_e = jnp.sum(jnp.where(lane == first_expert + e, meta, 0.0), axis=1, keepdims=True)
            hg = _dot(x, wg_ref[e])
            hu = _dot(x, wu_ref[e])
            act = (_silu(hg) * hu * w_e).astype(BF16)
            acc = acc + _dot(act, wd_ref[e])

        @pl.when(i >= 2)
        def _():
            wait_scatter(slot)

        ys_ref[slot] = acc
        if last:
            start_scatter(dst_ref, slot)
            wait_scatter(1 - slot)
            wait_scatter(slot)

    @pl.when(i == 0)
    def _():
        step(True, False)

    @pl.when((i > 0) & (i < last_step))
    def _():
        step(False, False)

    @pl.when(i == last_step)
    def _():
        step(False, True)


def _moe_plan(tx, counts, tm):
    t = tx.shape[0]
    group = tx[:, D_MODEL + META_GROUP_LANE].astype(jnp.int32)
    count = counts[0, :MOE_GROUPS].astype(jnp.int32)
    padded = (count + tm - 1) // tm * tm
    ends = jnp.cumsum(padded)
    starts = ends - padded
    first_of_group = jnp.cumsum(count) - count
    tiles = t // tm + MOE_GROUPS
    tile_start = jnp.arange(tiles, dtype=jnp.int32) * tm
    tile_group = jnp.sum((tile_start[:, None] >= ends[None, :]).astype(jnp.int32), axis=1)
    tile_group = jnp.minimum(tile_group, MOE_GROUPS - 1)
    order = jnp.argsort(group, stable=True).astype(jnp.int32)
    row_group = jnp.repeat(tile_group, tm)
    k = jnp.arange(tiles * tm, dtype=jnp.int32) - starts[row_group]
    is_real = k < count[row_group]
    src = jnp.where(is_real, order[jnp.clip(first_of_group[row_group] + k, 0, t - 1)], 0)
    spare = t - 1 + jnp.cumsum((~is_real).astype(jnp.int32))
    dst = jnp.where(is_real, src, spare)
    return tile_group, src.reshape(tiles, 1, tm), dst.reshape(tiles, 1, tm)


def _experts(tx, tile_group, src, dst, w_gate, w_up, w_down, layer):
    tiles, _, tm = src.shape
    assert tiles >= 3
    out_rows = tiles * tm
    idx_spec = lambda shift: pl.BlockSpec(
        (1, 1, tm), lambda i, tg: (jnp.clip(i + shift, 0, tiles - 1), 0, 0),
        memory_space=pltpu.SMEM)
    w_spec = lambda a, b: pl.BlockSpec((None, None, EXPERTS_PER_GROUP, a, b),
                                       lambda i, tg: (layer, tg[i], 0, 0, 0))
    grid_spec = pltpu.PrefetchScalarGridSpec(
        num_scalar_prefetch=1,
        grid=(tiles,),
        in_specs=[idx_spec(0), idx_spec(1), idx_spec(0), idx_spec(-1),
                  pl.BlockSpec(memory_space=pl.ANY),
                  w_spec(D_MODEL, D_EXPERT), w_spec(D_MODEL, D_EXPERT), w_spec(D_EXPERT, D_MODEL)],
        out_specs=pl.BlockSpec(memory_space=pl.ANY),
        scratch_shapes=[pltpu.VMEM((2, tm, D_MODEL + LANES), F32),
                        pltpu.VMEM((2, tm, D_MODEL), F32),
                        pltpu.SemaphoreType.DMA((2,)), pltpu.SemaphoreType.DMA((2,))],
    )
    return pl.pallas_call(
        functools.partial(_experts_kernel, tm=tm),
        grid_spec=grid_spec,
        out_shape=jax.ShapeDtypeStruct((out_rows, D_MODEL), F32),
        compiler_params=_cparams(("arbitrary",)),
        name="experts",
    )(tile_group, src, src, dst, dst, tx, w_gate, w_up, w_down)


def _ple_kernel(x_ref, moe_ref, pp_ref, ps_ref, g_ref, wg_ref, wp_ref, gf_ref, *out_refs,
                prompt_tiles, final):
    i = pl.program_id(0)
    x = x_ref[...] + moe_ref[...]
    h = _rmsnorm(x, g_ref[...]).astype(BF16)
    gate = _sigmoid(_dot(h, wg_ref[...]))
    p = jnp.where(i < prompt_tiles, pp_ref[...], ps_ref[...])
    out = x + gate * _dot(p.astype(BF16), wp_ref[...])
    results = [out, _rmsnorm(out, gf_ref[...])] if final else [out]

    @pl.when(i < prompt_tiles)
    def _():
        for n, val in enumerate(results):
            out_refs[2 * n][...] = val

    @pl.when(i == prompt_tiles)
    def _():
        for n, val in enumerate(results):
            out_refs[2 * n + 1][...] = val


def _ple(x, moe, p_prompt, p_sample, g, w_gate, w_proj, g_final, layer, final):
    t = x.shape[0]
    tm = TOKEN_TILE
    tp, ts = p_prompt.shape[1], p_sample.shape[1]
    prompt_tiles = tp // tm
    assert ts == tm and prompt_tiles + 1 == t // tm
    row = lambda width: pl.BlockSpec((tm, width), lambda i: (i, 0))
    par = lambda shape: pl.BlockSpec(shape, lambda i: (0,) * len(shape))
    p_idx = lambda i: jnp.minimum(i, prompt_tiles - 1)
    n_res = 2 if final else 1
    out_specs = [pl.BlockSpec((tm, D_MODEL), lambda i: (p_idx(i), 0)),
                 pl.BlockSpec((tm, D_MODEL), lambda i: (0, 0))] * n_res
    out_shape = [jax.ShapeDtypeStruct((tp, D_MODEL), F32),
                 jax.ShapeDtypeStruct((ts, D_MODEL), F32)] * n_res
    return pl.pallas_call(
        functools.partial(_ple_kernel, prompt_tiles=prompt_tiles, final=final),
        grid=(t // tm,),
        in_specs=[row(D_MODEL), row(D_MODEL),
                  pl.BlockSpec((None, tm, PLE_DIM), lambda i: (layer, p_idx(i), 0)),
                  pl.BlockSpec((None, tm, PLE_DIM), lambda i: (layer, 0, 0)),
                  par((1, D_MODEL)), par((D_MODEL, D_MODEL)), par((PLE_DIM, D_MODEL)),
                  par((1, D_MODEL))],
        out_specs=out_specs,
        out_shape=out_shape,
        compiler_params=_cparams(("arbitrary",)),
        name="ple",
    )(x, moe, p_prompt, p_sample, g, w_gate, w_proj, g_final)


def _pad_rows(a, rows):
    return jnp.pad(a, ((0, 0), (rows - a.shape[1], 0), (0, 0)))


def _layer(i, x_p, x_s, p_prompt, p_sample, cache_k, cache_v, page_table, state_ssm,
           state_ssm_conv, state_cf_conv, prm, dims, final):
    bp, sp_len, bs, ss_len = dims
    tp = bp * sp_len
    ts = bs * ss_len
    row = lambda v: v.reshape(1, -1)

    w_t = jnp.transpose(prm['w_in'][i])
    dt_rows = jnp.pad(w_t[OFF_DT:OFF_GLU], ((0, LANES - SSM_HEADS), (0, 0)))
    w_main_t = jnp.concatenate([w_t[:OFF_DT], w_t[OFF_GLU:OFF_GATE], dt_rows],
                               axis=0).astype(BF16)
    w_gates_t = w_t[OFF_GATE:].astype(BF16)
    g_mix = row(prm['norm_mix'][i])

    qb_p, kt_p, vt_p, ktb_p, vtb_p, z_p, xbc_p, glu_p, dt_p = _in_proj_prompt(
        x_p, g_mix, w_main_t, bp, sp_len)
    qb_s, k_s, v_s, kb_s, vb_s, z_s, xbc_s, glu_s, dt_s = _in_proj_rows(
        x_s, g_mix, w_main_t, 0, ts)

    a_prompt = _attn_prompt(qb_p, ktb_p, vtb_p, prm['sb_bias'][i], bp, sp_len)
    seq3 = lambda a: a.reshape(bs, ss_len, MIX_WIDTH)
    a_sample = _attn_decode(seq3(qb_s), seq3(kb_s), seq3(vb_s), cache_k, cache_v, page_table,
                            prm['sb_bias'][i], i).reshape(ts, MIX_WIDTH)

    pad_lanes = lambda vec, fill: jnp.pad(vec, (0, LANES - vec.shape[0]),
                                          constant_values=fill).reshape(1, LANES)
    ssd_prm = (prm['ssm_conv_w'][i], row(prm['ssm_conv_b'][i]), pad_lanes(prm['dt_bias'][i], 0.0),
               pad_lanes(-jnp.exp(prm['a_log'][i]), 0.0),
               row(jnp.repeat(prm['d_skip'][i], SSM_HEAD_DIM)), row(prm['ssm_norm'][i]))
    zeros_conv = jnp.zeros((bp, SUBLANES, SSM_CONV_CH), F32)
    zeros_h = jnp.zeros((bp, SSM_HEADS, SSM_HEAD_DIM, SSM_STATE), F32)
    b_prompt, sconv_p, h_p = _ssd(xbc_p, dt_p, z_p, zeros_conv, zeros_h, *ssd_prm,
                                  batch=bp, seq=sp_len, valid=sp_len)
    padseq = lambda a: jnp.pad(a.reshape(bs, ss_len, -1),
                               ((0, 0), (0, SSM_CHUNK - ss_len), (0, 0))).reshape(bs * SSM_CHUNK, -1)
    b_s, sconv_s, h_s = _ssd(padseq(xbc_s), padseq(dt_s), padseq(z_s),
                             _pad_rows(state_ssm_conv[i], SUBLANES), state_ssm[i], *ssd_prm,
                             batch=bs, seq=SSM_CHUNK, valid=ss_len)
    b_sample = b_s.reshape(bs, SSM_CHUNK, MIX_WIDTH)[:, :ss_len].reshape(ts, MIX_WIDTH)

    cf_prm = (prm['cf_conv_w'][i], row(prm['cf_conv_b'][i]), row(prm['cf_ln_g'][i]),
              row(prm['cf_ln_b'][i]))
    c_prompt, cf_p = _conformer(glu_p, jnp.zeros((bp, CF_HALO, MIX_WIDTH), F32), *cf_prm,
                                batch=bp, seq=sp_len)
    c_sample, cf_s = _conformer(glu_s, _pad_rows(state_cf_conv[i], CF_HALO), *cf_prm,
                                batch=bs, seq=ss_len)

    x = _merge(x_p, x_s, (a_prompt, b_prompt, c_prompt), (a_sample, b_sample, c_sample), g_mix,
               w_gates_t, prm['w_branch'][i].astype(BF16), prm['w_out'][i].astype(BF16))

    w_router = jnp.concatenate(
        [prm['w_router_expert'][i].reshape(D_MODEL, N_EXPERTS), prm['w_router_group'][i],
         jnp.zeros((D_MODEL, LANES - N_EXPERTS - MOE_GROUPS), F32)], axis=1)
    b_router = jnp.concatenate(
        [prm['b_router_expert'][i].reshape(N_EXPERTS), prm['b_router_group'][i],
         jnp.zeros((LANES - N_EXPERTS - MOE_GROUPS,), F32)]).reshape(1, LANES)
    tx, counts = _router(x, row(prm['norm_ffn'][i]), w_router, b_router)
    tile_group, src, dst = _moe_plan(tx, counts, MOE_ROW_TILE)
    moe = _experts(tx, tile_group, src, dst, prm['w_expert_gate_g'], prm['w_expert_up_g'],
                   prm['w_expert_down_g'], layer=i)

    outs = _ple(x, moe, p_prompt, p_sample, row(prm['norm_ple'][i]),
                prm['w_ple_gate'][i].astype(BF16), prm['w_ple_proj'][i].astype(BF16),
                row(prm['norm_final']), layer=i, final=final)

    conv_tail = SSM_CONV - 1
    cf_tail = CF_CONV - 1
    states = dict(
        k_p=jnp.transpose(kt_p.reshape(bp, SB_HEADS, SB_HEAD_DIM, sp_len), (0, 3, 1, 2)),
        v_p=jnp.transpose(vt_p.reshape(bp, SB_HEADS, SB_HEAD_DIM, sp_len), (0, 3, 1, 2)),
        h_p=h_p, sconv_p=sconv_p[:, SUBLANES - conv_tail:], cf_p=cf_p[:, CF_HALO - cf_tail:],
        k_s=k_s.reshape(bs, ss_len, SB_HEADS, SB_HEAD_DIM),
        v_s=v_s.reshape(bs, ss_len, SB_HEADS, SB_HEAD_DIM),
        h_s=h_s, sconv_s=sconv_s[:, SUBLANES - conv_tail:], cf_s=cf_s[:, CF_HALO - cf_tail:])
    return outs, states


def kernel(x_prompt, x_sample, p_prompt, p_sample, cache_k, cache_v, page_table, state_ssm, state_ssm_conv, state_cf_conv, norm_mix, w_in, sb_bias, ssm_conv_w, ssm_conv_b, dt_bias, a_log, d_skip, ssm_norm, cf_conv_w, cf_conv_b, cf_ln_g, cf_ln_b, w_branch, w_out, norm_ffn, w_router_group, b_router_group, w_router_expert, b_router_expert, w_expert_gate, w_expert_up, w_expert_down, norm_ple, w_ple_gate, w_ple_proj, norm_final):
    prm = dict(norm_mix=norm_mix, w_in=w_in, sb_bias=sb_bias, ssm_conv_w=ssm_conv_w,
               ssm_conv_b=ssm_conv_b, dt_bias=dt_bias, a_log=a_log, d_skip=d_skip,
               ssm_norm=ssm_norm, cf_conv_w=cf_conv_w, cf_conv_b=cf_conv_b, cf_ln_g=cf_ln_g,
               cf_ln_b=cf_ln_b, w_branch=w_branch, w_out=w_out, norm_ffn=norm_ffn,
               w_router_group=w_router_group, b_router_group=b_router_group,
               w_router_expert=w_router_expert, b_router_expert=b_router_expert,
               w_expert_gate=w_expert_gate, w_expert_up=w_expert_up,
               w_expert_down=w_expert_down, norm_ple=norm_ple, w_ple_gate=w_ple_gate,
               w_ple_proj=w_ple_proj, norm_final=norm_final)
    depth = w_in.shape[0]
    for name in ('w_expert_gate', 'w_expert_up', 'w_expert_down'):
        w = prm[name]
        prm[name + '_g'] = w.astype(BF16).reshape(
            (depth, MOE_GROUPS, EXPERTS_PER_GROUP) + w.shape[2:])
    bp, sp_len, _ = x_prompt.shape
    bs, ss_len, _ = x_sample.shape
    tp = bp * sp_len
    ts = bs * ss_len
    dims = (bp, sp_len, bs, ss_len)
    n_pool = cache_k.shape[1]
    ck = jnp.transpose(cache_k, (0, 1, 3, 4, 2)).reshape(depth, n_pool, MIX_WIDTH, PAGE_SIZE)
    cv = jnp.transpose(cache_v, (0, 1, 3, 4, 2)).reshape(depth, n_pool, MIX_WIDTH, PAGE_SIZE)

    x_p = x_prompt.reshape(tp, D_MODEL)
    x_s = x_sample.reshape(ts, D_MODEL)
    pp = p_prompt.reshape(depth, tp, PLE_DIM)
    ps = p_sample.reshape(depth, ts, PLE_DIM)
    per_layer = []
    for i in range(depth):
        outs, st = _layer(i, x_p, x_s, pp, ps, ck, cv, page_table, state_ssm, state_ssm_conv,
                          state_cf_conv, prm, dims, final=(i == depth - 1))
        x_p, x_s = outs[0], outs[1]
        per_layer.append(st)
    y_p, y_s = outs[2], outs[3]
    stack = lambda name: jnp.stack([st[name] for st in per_layer])
    return (y_p.reshape(bp, sp_len, D_MODEL), y_s.reshape(bs, ss_len, D_MODEL),
            stack('k_p'), stack('v_p'), stack('h_p'), stack('sconv_p'), stack('cf_p'),
            stack('k_s'), stack('v_s'), stack('h_s'), stack('sconv_s'), stack('cf_s'))
```

```python
import functools

import jax
import jax.numpy as jnp
from jax import lax
from jax.experimental import pallas as pl
from jax.experimental.pallas import tpu as pltpu

F32 = jnp.float32
BF16 = jnp.bfloat16
EPS = 1e-6

D_MODEL = 1024
MIX_WIDTH = 512
N_BRANCH = 3
SB_HEADS = 8
SB_HEAD_DIM = 64
SSM_HEADS = 8
SSM_HEAD_DIM = 64
SSM_GROUPS = 2
SSM_STATE = 128
SSM_CONV = 4
SSM_CONV_CH = MIX_WIDTH + 2 * SSM_GROUPS * SSM_STATE
SSM_CHUNK = 128
CF_CONV = 31
MOE_GROUPS = 4
EXPERTS_PER_GROUP = 4
N_EXPERTS = MOE_GROUPS * EXPERTS_PER_GROUP
D_EXPERT = 512
PLE_DIM = 256
PAGE_SIZE = 128
OFF_DT = 3 * MIX_WIDTH + MIX_WIDTH + SSM_CONV_CH
OFF_GLU = OFF_DT + SSM_HEADS
OFF_GATE = OFF_GLU + 2 * MIX_WIDTH

LANES = 128
SUBLANES = 8
VMEM_LIMIT = 56 * 1024 * 1024

TOKEN_TILE = 256
MOE_ROW_TILE = 256
ROUTER_TILE = 640
SSM_SEQS_PER_STEP = 4
ATTN_TILE = 256
ATTN_GROUP = 4
CF_TILE = 256
DEC_PAGES_PER_STEP = 16


def _cparams(sem):
    return pltpu.CompilerParams(dimension_semantics=sem, vmem_limit_bytes=VMEM_LIMIT)


def _rmsnorm(x, g):
    return x * lax.rsqrt(jnp.mean(x * x, axis=-1, keepdims=True) + EPS) * g


def _sigmoid(x):
    return 1.0 / (1.0 + jnp.exp(-x))


def _silu(x):
    return x * _sigmoid(x)


def _softplus(x):
    return jnp.maximum(x, 0.0) + jnp.log(1.0 + jnp.exp(-jnp.abs(x)))


LOG2E = 1.4426950408889634


def _softplus2(x):
    return jnp.maximum(x, 0.0) + jnp.log2(1.0 + jnp.exp2(-jnp.abs(x)))


def _dot(a, b):
    return jnp.dot(a, b, preferred_element_type=F32)


def _dot_nt(a, b):
    return lax.dot_general(a, b, (((1,), (1,)), ((), ())), preferred_element_type=F32)


def _dot_tn(a, b):
    return lax.dot_general(a, b, (((0,), (0,)), ((), ())), preferred_element_type=F32)


def _split3(x):
    hi = x.astype(BF16)
    r = x - hi.astype(F32)
    mid = r.astype(BF16)
    lo = (r - mid.astype(F32)).astype(BF16)
    return hi, mid, lo


def _dot_exact_lhs(mat_bf16, x):
    hi, mid, lo = _split3(x)
    return _dot(mat_bf16, hi) + _dot(mat_bf16, mid) + _dot(mat_bf16, lo)


IN_PROJ_WIDTHS = [MIX_WIDTH, MIX_WIDTH, MIX_WIDTH, MIX_WIDTH, MIX_WIDTH, MIX_WIDTH,
                  SSM_CONV_CH, 2 * MIX_WIDTH, LANES]
IN_PROJ_DTYPES = [BF16, F32, F32, BF16, BF16, F32, F32, F32, F32]


def _in_proj_kernel(x_ref, g_ref, w_ref, q_ref, k_ref, v_ref, kb_ref, vb_ref,
                    z_ref, xbc_ref, glu_ref, dt_ref, *, transposed_kv):
    h = _rmsnorm(x_ref[...], g_ref[...]).astype(BF16)
    w = MIX_WIDTH
    if transposed_kv:
        q = _dot_nt(h, w_ref[0:w, :])
        kv = _dot_nt(w_ref[w:3 * w, :], h)
        k = kv[0:w]
        v = kv[w:2 * w]
        rest = _dot_nt(h, w_ref[3 * w:, :])
    else:
        p = _dot_nt(h, w_ref[...])
        q = p[:, 0:w]
        k = p[:, w:2 * w]
        v = p[:, 2 * w:3 * w]
        rest = p[:, 3 * w:]
    q_ref[...] = (q * (SB_HEAD_DIM ** -0.5 * LOG2E)).astype(BF16)
    k_ref[...] = k
    v_ref[...] = v
    kb_ref[...] = k.astype(BF16)
    vb_ref[...] = v.astype(BF16)
    z_ref[...] = rest[:, 0:w]
    xbc_ref[...] = rest[:, w:3 * w]
    glu_ref[...] = rest[:, 3 * w:5 * w]
    dt_ref[...] = rest[:, 5 * w:5 * w + LANES]


def _in_proj_prompt(x, g, w_main_t, batch, seq):
    tm = TOKEN_TILE
    nt = seq // tm
    n = w_main_t.shape[0]
    row = lambda width: pl.BlockSpec((tm, width), lambda b, j: (b * nt + j, 0))
    par = lambda shape: pl.BlockSpec(shape, lambda b, j: (0,) * len(shape))
    out_specs = [row(wd) for wd in IN_PROJ_WIDTHS]
    out_shape = [jax.ShapeDtypeStruct((batch * seq, wd), dt)
                 for wd, dt in zip(IN_PROJ_WIDTHS, IN_PROJ_DTYPES)]
    for idx in (1, 2):
        out_specs[idx] = pl.BlockSpec((None, MIX_WIDTH, tm), lambda b, j: (b, 0, j))
        out_shape[idx] = jax.ShapeDtypeStruct((batch, MIX_WIDTH, seq), F32)
    for idx in (3, 4):
        out_specs[idx] = pl.BlockSpec((None, None, MIX_WIDTH, tm), lambda b, j: (b, j, 0, 0))
        out_shape[idx] = jax.ShapeDtypeStruct((batch, nt, MIX_WIDTH, tm), BF16)
    return pl.pallas_call(
        functools.partial(_in_proj_kernel, transposed_kv=True),
        grid=(batch, nt),
        in_specs=[row(D_MODEL), par((1, D_MODEL)), par((n, D_MODEL))],
        out_specs=out_specs,
        out_shape=out_shape,
        compiler_params=_cparams(("parallel", "parallel")),
        name="in_proj_prompt",
    )(x, g, w_main_t)


def _in_proj_rows(x, g, w_main_t, row_start, rows):
    tm = TOKEN_TILE
    first = row_start // tm
    n = w_main_t.shape[0]
    par = lambda shape: pl.BlockSpec(shape, lambda i: (0,) * len(shape))
    return pl.pallas_call(
        functools.partial(_in_proj_kernel, transposed_kv=False),
        grid=(rows // tm,),
        in_specs=[pl.BlockSpec((tm, D_MODEL), lambda i: (i + first, 0)),
                  par((1, D_MODEL)), par((n, D_MODEL))],
        out_specs=[pl.BlockSpec((tm, wd), lambda i: (i, 0)) for wd in IN_PROJ_WIDTHS],
        out_shape=[jax.ShapeDtypeStruct((rows, wd), dt)
                   for wd, dt in zip(IN_PROJ_WIDTHS, IN_PROJ_DTYPES)],
        compiler_params=_cparams(("parallel",)),
        name="in_proj_rows",
    )(x, g, w_main_t)


def _strict_upper(n):
    r = lax.broadcasted_iota(jnp.int32, (n, n), 0)
    c = lax.broadcasted_iota(jnp.int32, (n, n), 1)
    return jnp.where(r > c, 1.0, 0.0).astype(BF16)


def _sb_block(z, v_blk, upper, carry, mask, v_channel_major=False):
    sp = _softplus2(z)
    if mask is not None:
        sp = jnp.where(mask, sp, 0.0)
    after = _dot(sp.astype(BF16), upper)
    w = jnp.exp2(z - sp - after - carry)
    if mask is not None:
        w = jnp.where(mask, w, 0.0)
    pv = _dot_nt if v_channel_major else _dot
    return pv(w.astype(BF16), v_blk), jnp.sum(sp, axis=1, keepdims=True)


def _attn_prompt_kernel(bias_ref, q_ref, k_ref, v_ref, o_ref, acc_ref, carry_ref, *, seq, tile):
    hp = pl.program_id(1)
    n_q = seq // tile
    upper = _strict_upper(tile)
    lane = lax.broadcasted_iota(jnp.int32, (tile, LANES), 1)
    head0 = lane < SB_HEAD_DIM
    r = lax.broadcasted_iota(jnp.int32, (tile, tile), 0)
    c = lax.broadcasted_iota(jnp.int32, (tile, tile), 1)
    diag_mask = jnp.concatenate([c < r, c < r], axis=0)
    bias0 = bias_ref[hp * 2] * LOG2E
    bias1 = bias_ref[hp * 2 + 1] * LOG2E

    def visit(pairs, mask):
        qis = [qi for qi, _ in pairs]
        kjs = [kj for _, kj in pairs]
        zs, sps, afters, carries = [], [], [], []
        for qi, kj in pairs:
            q = q_ref[qi * tile:(qi + 1) * tile, :]
            zero = jnp.zeros_like(q)
            q2 = jnp.concatenate([jnp.where(head0, q, zero), jnp.where(head0, zero, q)], axis=0)
            d = _dot(q2, k_ref[kj])
            zs.append(jnp.concatenate([d[:tile] + bias0, d[tile:] + bias1], axis=0))
        for z in zs:
            sp = _softplus2(z)
            sps.append(sp if mask is None else jnp.where(mask, sp, 0.0))
        for sp in sps:
            afters.append(_dot(sp.astype(BF16), upper))
        contribs = []
        for qi, kj, z, sp, after in zip(qis, kjs, zs, sps, afters):
            if mask is None:
                carry = carry_ref[qi]
                carries.append(carry)
                after = after + jnp.concatenate([carry] * (tile // LANES), axis=1)
            w = jnp.exp2(z - sp - after)
            if mask is not None:
                w = jnp.where(mask, w, 0.0)
            contribs.append(_dot_nt(w.astype(BF16), v_ref[kj]))
        for n, (qi, sp, contrib) in enumerate(zip(qis, sps, contribs)):
            row_sum = jnp.broadcast_to(jnp.sum(sp, axis=1, keepdims=True), (2 * tile, LANES))
            if mask is None:
                acc_ref[qi] += contrib
                carry_ref[qi] = carries[n] + row_sum
            else:
                acc_ref[qi] = contrib
                carry_ref[qi] = row_sum

    for d in range(n_q):
        wave = [(qi, qi - d) for qi in range(d, n_q)]
        n_groups = -(-len(wave) // ATTN_GROUP)
        size = -(-len(wave) // n_groups)
        for g in range(n_groups):
            visit(wave[g * size:(g + 1) * size], diag_mask if d == 0 else None)

    for qi in range(n_q):
        acc = acc_ref[qi]
        o_ref[qi * tile:(qi + 1) * tile, :] = jnp.where(head0, acc[:tile], acc[tile:])


def _attn_prompt(q, kt, vt, sb_bias, batch, seq):
    tile = kt.shape[-1]
    nt = seq // tile
    blk = pl.BlockSpec((seq, LANES), lambda b, hp: (b, hp))
    kv_blk = pl.BlockSpec((None, nt, LANES, tile), lambda b, hp: (b, 0, hp, 0))
    return pl.pallas_call(
        functools.partial(_attn_prompt_kernel, seq=seq, tile=tile),
        grid=(batch, SB_HEADS // 2),
        in_specs=[pl.BlockSpec(memory_space=pltpu.SMEM), blk, kv_blk, kv_blk],
        out_specs=blk,
        out_shape=jax.ShapeDtypeStruct((batch * seq, MIX_WIDTH), F32),
        scratch_shapes=[pltpu.VMEM((nt, 2 * tile, LANES), F32),
                        pltpu.VMEM((nt, 2 * tile, LANES), F32)],
        compiler_params=_cparams(("parallel", "parallel")),
        name="attn_prompt",
    )(sb_bias, q, kt, vt)


def _attn_decode_kernel(pt_ref, bias_ref, q_ref, kn_ref, vn_ref, *rest, pages, dec_seq):
    k_refs = rest[:pages]
    v_refs = rest[pages:2 * pages]
    o_ref = rest[2 * pages]
    acc_ref, carry_ref = rest[2 * pages + 1:]
    step = pl.program_id(1)
    rows = SB_HEADS * dec_seq
    width = MIX_WIDTH
    upper = _strict_upper(PAGE_SIZE)
    r_w = lax.broadcasted_iota(jnp.int32, (rows, width), 0)
    c_w = lax.broadcasted_iota(jnp.int32, (rows, width), 1)
    head_mask = (r_w // dec_seq) == (c_w // SB_HEAD_DIM)
    q_rep = jnp.concatenate([q_ref[0].astype(F32)] * SB_HEADS, axis=0)
    q_bd = jnp.where(head_mask, q_rep, 0.0).astype(BF16)
    r_k = lax.broadcasted_iota(jnp.int32, (rows, PAGE_SIZE), 0)
    c_k = lax.broadcasted_iota(jnp.int32, (rows, PAGE_SIZE), 1)
    bias = jnp.zeros((rows, PAGE_SIZE), F32)
    for h in range(SB_HEADS):
        bias = jnp.where(r_k // dec_seq == h, bias_ref[h] * LOG2E, bias)

    @pl.when(step == 0)
    def _():
        pad = jnp.zeros((PAGE_SIZE - dec_seq, width), F32)
        kn = jnp.concatenate([kn_ref[0].astype(F32), pad], axis=0).astype(BF16)
        vn = jnp.concatenate([vn_ref[0].astype(F32), pad], axis=0).astype(BF16)
        z = _dot_nt(q_bd, kn) + bias
        mask = c_k < (r_k % dec_seq)
        contrib, rs = _sb_block(z, vn, upper, 0.0, mask)
        acc_ref[...] = contrib
        carry_ref[...] = jnp.broadcast_to(rs, (rows, PAGE_SIZE))

    k_all = jnp.concatenate([k_refs[j][...].astype(BF16) for j in range(pages)], axis=1)
    z = _dot(q_bd, k_all)
    z = jnp.concatenate([z[:, j * PAGE_SIZE:(j + 1) * PAGE_SIZE] + bias for j in range(pages)],
                        axis=0)
    sp = _softplus2(z)
    after = _dot(sp.astype(BF16), upper)
    row_sum = jnp.sum(sp, axis=1, keepdims=True)
    carry = carry_ref[...]
    carries = []
    for j in range(pages):
        carries.append(carry)
        carry = carry + row_sum[j * rows:(j + 1) * rows]
    w = jnp.exp2(z - sp - after - jnp.concatenate(carries, axis=0)).astype(BF16)
    w_all = jnp.concatenate([w[j * rows:(j + 1) * rows] for j in range(pages)], axis=1)
    v_all = jnp.concatenate([v_refs[j][...].astype(BF16) for j in range(pages)], axis=1)
    acc = acc_ref[...] + _dot_nt(w_all, v_all)
    acc_ref[...] = acc
    carry_ref[...] = carry

    @pl.when(step == pl.num_programs(1) - 1)
    def _():
        a = jnp.where(head_mask, acc, 0.0)
        out = a[0:dec_seq]
        for h in range(1, SB_HEADS):
            out = out + a[h * dec_seq:(h + 1) * dec_seq]
        o_ref[0] = out


def _attn_decode(q, k_new, v_new, cache_k, cache_v, page_table, sb_bias, layer):
    bsz, dec_seq, width = q.shape
    n_pages = page_table.shape[1]
    pages = DEC_PAGES_PER_STEP
    assert n_pages % pages == 0 and dec_seq == SUBLANES
    steps = n_pages // pages
    rows = SB_HEADS * dec_seq

    def page_spec(j):
        def imap(b, s, pt):
            return (layer, pt[b * n_pages + (n_pages - 1 - (s * pages + j))], 0, 0)
        return pl.BlockSpec((None, None, width, PAGE_SIZE), imap)

    seq_spec = pl.BlockSpec((1, dec_seq, width), lambda b, s, pt: (b, 0, 0))
    grid_spec = pltpu.PrefetchScalarGridSpec(
        num_scalar_prefetch=1,
        grid=(bsz, steps),
        in_specs=[pl.BlockSpec(memory_space=pltpu.SMEM), seq_spec, seq_spec, seq_spec]
        + [page_spec(j) for j in range(pages)] * 2,
        out_specs=seq_spec,
        scratch_shapes=[pltpu.VMEM((rows, width), F32), pltpu.VMEM((rows, PAGE_SIZE), F32)],
    )
    return pl.pallas_call(
        functools.partial(_attn_decode_kernel, pages=pages, dec_seq=dec_seq),
        grid_spec=grid_spec,
        out_shape=jax.ShapeDtypeStruct((bsz, dec_seq, width), F32),
        compiler_params=_cparams(("parallel", "arbitrary")),
        name="attn_decode",
    )(page_table.reshape(-1), sb_bias, q, k_new, v_new,
      *([cache_k] * pages), *([cache_v] * pages))


def _expand_heads(cols, width):
    src = lax.broadcasted_iota(jnp.int32, (LANES, width), 0)
    dst = lax.broadcasted_iota(jnp.int32, (LANES, width), 1)
    onehot = jnp.where(src == dst // SSM_HEAD_DIM, 1.0, 0.0).astype(BF16)
    hi, mid, lo = _split3(cols)
    return _dot(hi, onehot) + _dot(mid, onehot) + _dot(lo, onehot)


def _ssd_kernel(xbc_ref, dt_ref, z_ref, buf_ref, h0_ref, cw_ref, cb_ref, dtb_ref, a_ref,
                dskip_ref, ng_ref, y_ref, conv_ref, h_ref, xp_ref, hs_ref, *, chunk, valid):
    c = pl.program_id(1)
    n_seq = xbc_ref.shape[0]

    @pl.when(c == 0)
    def _():
        for g in range(n_seq):
            xp_ref[g, 0:SUBLANES, :] = buf_ref[g]
            hs_ref[g] = h0_ref[g]

    for g in range(n_seq):
        _ssd_chunk(xbc_ref.at[g], dt_ref.at[g], z_ref.at[g], cw_ref, cb_ref, dtb_ref, a_ref,
                   dskip_ref, ng_ref, y_ref.at[g], conv_ref.at[g], h_ref.at[g], xp_ref.at[g],
                   hs_ref.at[g], chunk, valid)


def _ssd_chunk(xbc_ref, dt_ref, z_ref, cw_ref, cb_ref, dtb_ref, a_ref, dskip_ref, ng_ref,
               y_ref, conv_ref, h_ref, xp_ref, hs_ref, chunk, valid):
    q = chunk
    halo = SUBLANES
    xp_ref[halo:halo + q, :] = xbc_ref[...]
    acc = cb_ref[...] + cw_ref[SSM_CONV - 1:SSM_CONV, :] * xp_ref[halo:halo + q, :]
    for j in range(SSM_CONV - 1):
        off = halo - (SSM_CONV - 1) + j
        acc = acc + cw_ref[j:j + 1, :] * xp_ref[off:off + q, :]
    last = min(valid, q)
    tail = xp_ref[last:last + halo, :]
    conv_ref[...] = tail
    xp_ref[0:halo, :] = tail
    xc = _silu(acc)
    xs = xc[:, 0:MIX_WIDTH]
    gw = SSM_GROUPS * SSM_STATE
    bm = xc[:, MIX_WIDTH:MIX_WIDTH + gw]
    cm = xc[:, MIX_WIDTH + gw:MIX_WIDTH + 2 * gw]

    dt = _softplus(dt_ref[...] + dtb_ref[...])
    if valid < q:
        row = lax.broadcasted_iota(jnp.int32, (q, LANES), 0)
        dt = jnp.where(row < valid, dt, 0.0)
    da = dt * a_ref[...]
    r = lax.broadcasted_iota(jnp.int32, (q, q), 0)
    s = lax.broadcasted_iota(jnp.int32, (q, q), 1)
    causal = s <= r
    tril = jnp.where(causal, 1.0, 0.0).astype(BF16)
    cum = _dot_exact_lhs(tril, da)
    cum_t = jnp.transpose(cum)
    dt_t = jnp.transpose(dt)
    cum_last = cum[q - 1:q, :]
    w_end = jnp.exp(cum_last - cum) * dt
    e_cum = jnp.exp(cum)

    hpg = SSM_HEADS // SSM_GROUPS
    gwid = hpg * SSM_HEAD_DIM
    lane_g = lax.broadcasted_iota(jnp.int32, (q, gwid), 1)
    xs_b = xs.astype(BF16)
    e_cum_x = _expand_heads(e_cum, MIX_WIDTH)
    w_end_x = _expand_heads(w_end, MIX_WIDTH)
    xw = (xs * w_end_x).astype(BF16)
    y_parts = []
    for g in range(SSM_GROUPS):
        b_g = bm[:, g * SSM_STATE:(g + 1) * SSM_STATE].astype(BF16)
        c_g = cm[:, g * SSM_STATE:(g + 1) * SSM_STATE].astype(BF16)
        cb = _dot_nt(c_g, b_g)
        xs_g = xs_b[:, g * gwid:(g + 1) * gwid]
        y_g = jnp.zeros((q, gwid), F32)
        for e in range(hpg):
            hd = g * hpg + e
            seg = cum[:, hd:hd + 1] - cum_t[hd:hd + 1, :]
            decay = jnp.where(causal, jnp.exp(jnp.where(causal, seg, 0.0)), 0.0)
            w = (cb * decay * dt_t[hd:hd + 1, :]).astype(BF16)
            y_g = jnp.where(lane_g // SSM_HEAD_DIM == e, _dot(w, xs_g), y_g)
        h_g = hs_ref[g * hpg:(g + 1) * hpg].reshape(gwid, SSM_STATE)
        y_prev = _dot_nt(c_g, h_g.astype(BF16))
        y_parts.append(y_g + y_prev * e_cum_x[:, g * gwid:(g + 1) * gwid])
        upd = _dot_tn(xw[:, g * gwid:(g + 1) * gwid], b_g)
        for e in range(hpg):
            hd = g * hpg + e
            scale = jnp.exp(cum_last[:, hd:hd + 1])
            hs_ref[hd] = hs_ref[hd] * scale + upd[e * SSM_HEAD_DIM:(e + 1) * SSM_HEAD_DIM, :]
    y = jnp.concatenate(y_parts, axis=1) + dskip_ref[...] * xs
    y = y * _silu(z_ref[...])
    y_ref[...] = _rmsnorm(y, ng_ref[...])
    h_ref[...] = hs_ref[...]


def _ssd(xbc, dt_raw, z, conv_buf, h0, conv_w, conv_b, dt_bias, a_neg, d_skip_row, norm_g,
         batch, seq, valid):
    q = SSM_CHUNK
    nc = seq // q
    ns = SSM_SEQS_PER_STEP if batch % SSM_SEQS_PER_STEP == 0 else 1
    seqs = lambda a: a.reshape(batch, seq, a.shape[-1])
    row = lambda width: pl.BlockSpec((ns, q, width), lambda b, c: (b, c, 0))
    par = lambda shape: pl.BlockSpec(shape, lambda b, c: (0,) * len(shape))
    buf_spec = pl.BlockSpec((ns, SUBLANES, SSM_CONV_CH), lambda b, c: (b, 0, 0))
    h_spec = pl.BlockSpec((ns, SSM_HEADS, SSM_HEAD_DIM, SSM_STATE), lambda b, c: (b, 0, 0, 0))
    y, conv_new, h_new = pl.pallas_call(
        functools.partial(_ssd_kernel, chunk=q, valid=valid),
        grid=(batch // ns, nc),
        in_specs=[row(SSM_CONV_CH), row(LANES), row(MIX_WIDTH), buf_spec, h_spec,
                  par((SSM_CONV, SSM_CONV_CH)), par((1, SSM_CONV_CH)), par((1, LANES)),
                  par((1, LANES)), par((1, MIX_WIDTH)), par((1, MIX_WIDTH))],
        out_specs=[row(MIX_WIDTH), buf_spec, h_spec],
        out_shape=[jax.ShapeDtypeStruct((batch, seq, MIX_WIDTH), F32),
                   jax.ShapeDtypeStruct((batch, SUBLANES, SSM_CONV_CH), F32),
                   jax.ShapeDtypeStruct((batch, SSM_HEADS, SSM_HEAD_DIM, SSM_STATE), F32)],
        scratch_shapes=[pltpu.VMEM((ns, q + 2 * SUBLANES, SSM_CONV_CH), F32),
                        pltpu.VMEM((ns, SSM_HEADS, SSM_HEAD_DIM, SSM_STATE), F32)],
        compiler_params=_cparams(("parallel", "arbitrary")),
        name="ssd",
    )(seqs(xbc), seqs(dt_raw), seqs(z), conv_buf, h0, conv_w, conv_b, dt_bias, a_neg,
      d_skip_row, norm_g)
    return y.reshape(batch * seq, MIX_WIDTH), conv_new, h_new


CF_HALO = 32


def _conformer_kernel(glu_ref, buf_ref, cw_ref, cb_ref, lg_ref, lb_ref, y_ref, new_ref,
                      xp_ref, *, tile):
    c = pl.program_id(1)

    @pl.when(c == 0)
    def _():
        xp_ref[0:CF_HALO, :] = buf_ref[0]
        xp_ref[CF_HALO + tile:, :] = jnp.zeros((SUBLANES, MIX_WIDTH), F32)

    glu = glu_ref[...]
    u = glu[:, 0:MIX_WIDTH] * _sigmoid(glu[:, MIX_WIDTH:2 * MIX_WIDTH])
    xp_ref[CF_HALO:CF_HALO + tile, :] = u
    first = CF_HALO - (CF_CONV - 1)
    acc = cb_ref[...]
    for b in range(SUBLANES):
        part = None
        for a in range((first + CF_CONV - 1) // SUBLANES + 1):
            j = a * SUBLANES + b - first
            if 0 <= j < CF_CONV:
                term = cw_ref[j:j + 1, :] * xp_ref[a * SUBLANES:a * SUBLANES + tile + SUBLANES, :]
                part = term if part is None else part + term
        acc = acc + part[b:b + tile]
    tail = xp_ref[tile:tile + CF_HALO, :]
    new_ref[0] = tail
    xp_ref[0:CF_HALO, :] = tail
    mu = jnp.mean(acc, axis=-1, keepdims=True)
    xc = acc - mu
    var = jnp.mean(xc * xc, axis=-1, keepdims=True)
    y_ref[...] = _silu(xc * lax.rsqrt(var + EPS) * lg_ref[...] + lb_ref[...])


def _conformer(glu, conv_buf, conv_w, conv_b, ln_g, ln_b, batch, seq):
    tile = CF_TILE if seq % CF_TILE == 0 else seq
    nt = seq // tile
    par = lambda shape: pl.BlockSpec(shape, lambda b, c: (0,) * len(shape))
    buf_spec = pl.BlockSpec((1, CF_HALO, MIX_WIDTH), lambda b, c: (b, 0, 0))
    return pl.pallas_call(
        functools.partial(_conformer_kernel, tile=tile),
        grid=(batch, nt),
        in_specs=[pl.BlockSpec((tile, 2 * MIX_WIDTH), lambda b, c: (b * nt + c, 0)), buf_spec,
                  par((CF_CONV, MIX_WIDTH)), par((1, MIX_WIDTH)), par((1, MIX_WIDTH)),
                  par((1, MIX_WIDTH))],
        out_specs=[pl.BlockSpec((tile, MIX_WIDTH), lambda b, c: (b * nt + c, 0)), buf_spec],
        out_shape=[jax.ShapeDtypeStruct((batch * seq, MIX_WIDTH), F32),
                   jax.ShapeDtypeStruct((batch, CF_HALO, MIX_WIDTH), F32)],
        scratch_shapes=[pltpu.VMEM((tile + CF_HALO + SUBLANES, MIX_WIDTH), F32)],
        compiler_params=_cparams(("parallel", "arbitrary")),
        name="conformer",
    )(glu, conv_buf, conv_w, conv_b, ln_g, ln_b)


def _merge_kernel(xp_ref, xs_ref, ap_ref, bp_ref, cp_ref, as_ref, bs_ref, cs_ref, g_ref, wg_ref,
                  wb_ref, wo_ref, o_ref, *, prompt_tiles):
    is_prompt = pl.program_id(0) < prompt_tiles
    x = jnp.where(is_prompt, xp_ref[...], xs_ref[...])
    h = _rmsnorm(x, g_ref[...]).astype(BF16)
    merged = jnp.zeros(x.shape, F32)
    for n, (br_p, br_s) in enumerate(((ap_ref, as_ref), (bp_ref, bs_ref), (cp_ref, cs_ref))):
        br = jnp.where(is_prompt, br_p[...], br_s[...])
        gate = _sigmoid(_dot_nt(h, wg_ref[n * D_MODEL:(n + 1) * D_MODEL, :]))
        merged = merged + gate * _dot(br.astype(BF16), wb_ref[n])
    o_ref[...] = x + _dot(merged.astype(BF16), wo_ref[...])


def _merge(x_p, x_s, prompt_branches, sample_branches, g, w_gates, w_branch, w_out):
    tm = TOKEN_TILE
    prompt_tiles = x_p.shape[0] // tm
    assert x_s.shape[0] == tm and sample_branches[0].shape[0] == tm
    p_row = lambda width: pl.BlockSpec((tm, width),
                                       lambda i: (jnp.minimum(i, prompt_tiles - 1), 0))
    s_row = lambda width: pl.BlockSpec((tm, width), lambda i: (0, 0))
    par = lambda shape: pl.BlockSpec(shape, lambda i: (0,) * len(shape))
    return pl.pallas_call(
        functools.partial(_merge_kernel, prompt_tiles=prompt_tiles),
        grid=(prompt_tiles + 1,),
        in_specs=[p_row(D_MODEL), s_row(D_MODEL)] + [p_row(MIX_WIDTH)] * 3 + [s_row(MIX_WIDTH)] * 3
        + [par((1, D_MODEL)), par((N_BRANCH * D_MODEL, D_MODEL)),
           par((N_BRANCH, MIX_WIDTH, D_MODEL)), par((D_MODEL, D_MODEL))],
        out_specs=pl.BlockSpec((tm, D_MODEL), lambda i: (i, 0)),
        out_shape=jax.ShapeDtypeStruct(((prompt_tiles + 1) * tm, D_MODEL), F32),
        compiler_params=_cparams(("parallel",)),
        name="merge",
    )(x_p, x_s, *prompt_branches, *sample_branches, g, w_gates, w_branch, w_out)


META_GROUP_LANE = N_EXPERTS


def _router_kernel(x_ref, g_ref, wr_ref, br_ref, tx_ref, cnt_ref, run_ref):
    i = pl.program_id(0)

    @pl.when(i == 0)
    def _():
        run_ref[...] = jnp.zeros_like(run_ref)

    t = _rmsnorm(x_ref[...], g_ref[...])
    tx_ref[:, 0:D_MODEL] = t
    logits = jnp.dot(t, wr_ref[...], preferred_element_type=F32,
                     precision=lax.Precision.HIGHEST) + br_ref[...]
    lane = lax.broadcasted_iota(jnp.int32, logits.shape, 1)
    neg = -jnp.inf
    is_group = (lane >= N_EXPERTS) & (lane < N_EXPERTS + MOE_GROUPS)
    gl = jnp.where(is_group, logits, neg)
    gmax = jnp.max(gl, axis=1, keepdims=True)
    g_idx = jnp.min(jnp.where(gl == gmax, lane, LANES), axis=1, keepdims=True) - N_EXPERTS
    g_w = 1.0 / jnp.sum(jnp.where(is_group, jnp.exp(logits - gmax), 0.0), axis=1, keepdims=True)
    in_group = (lane // EXPERTS_PER_GROUP) == g_idx
    el = jnp.where(in_group, logits, neg)
    v1 = jnp.max(el, axis=1, keepdims=True)
    i1 = jnp.min(jnp.where(el == v1, lane, LANES), axis=1, keepdims=True)
    el2 = jnp.where(lane == i1, neg, el)
    v2 = jnp.max(el2, axis=1, keepdims=True)
    i2 = jnp.min(jnp.where(el2 == v2, lane, LANES), axis=1, keepdims=True)
    e21 = jnp.exp(v2 - v1)
    p1 = 1.0 / (1.0 + e21)
    p2 = e21 * p1
    comb = jnp.where(lane == i1, p1 * g_w, 0.0) + jnp.where(lane == i2, p2 * g_w, 0.0)
    onehot = jnp.where(lane == g_idx, 1.0, 0.0)
    run = run_ref[...] + jnp.sum(onehot, axis=0, keepdims=True)
    run_ref[...] = run
    cnt_ref[...] = run
    tx_ref[:, D_MODEL:] = jnp.where(lane == META_GROUP_LANE, g_idx.astype(F32), comb)


def _router(x, g, w_router, b_router):
    t = x.shape[0]
    tm = ROUTER_TILE if t % ROUTER_TILE == 0 else TOKEN_TILE
    row = lambda width: pl.BlockSpec((tm, width), lambda i: (i, 0))
    par = lambda shape: pl.BlockSpec(shape, lambda i: (0,) * len(shape))
    return pl.pallas_call(
        _router_kernel,
        grid=(t // tm,),
        in_specs=[row(D_MODEL), par((1, D_MODEL)), par((D_MODEL, LANES)), par((1, LANES))],
        out_specs=[row(D_MODEL + LANES), par((1, LANES))],
        out_shape=[jax.ShapeDtypeStruct((t, D_MODEL + LANES), F32),
                   jax.ShapeDtypeStruct((1, LANES), F32)],
        scratch_shapes=[pltpu.VMEM((1, LANES), F32)],
        compiler_params=_cparams(("arbitrary",)),
        name="router",
    )(x, g, w_router, b_router)


def _experts_kernel(tile_group_ref, src_ref, src_next_ref, dst_ref, dst_prev_ref, tx_hbm,
                    wg_ref, wu_ref, wd_ref, y_hbm, xs_ref, ys_ref, gsem, ssem, *, tm):
    i = pl.program_id(0)
    last_step = pl.num_programs(0) - 1
    slot = i % 2

    def start_gather(idx_ref, s):
        for r in range(tm):
            pltpu.make_async_copy(tx_hbm.at[pl.ds(idx_ref[0, 0, r], 1)],
                                  xs_ref.at[s, pl.ds(r, 1)], gsem.at[s]).start()

    def start_scatter(idx_ref, s):
        for r in range(tm):
            pltpu.make_async_copy(ys_ref.at[s, pl.ds(r, 1)],
                                  y_hbm.at[pl.ds(idx_ref[0, 0, r], 1)], ssem.at[s]).start()

    def wait_gather(s):
        pltpu.make_async_copy(tx_hbm.at[pl.ds(0, tm)], xs_ref.at[s], gsem.at[s]).wait()

    def wait_scatter(s):
        pltpu.make_async_copy(ys_ref.at[s], y_hbm.at[pl.ds(0, tm)], ssem.at[s]).wait()

    def step(first, last):
        if first:
            start_gather(src_ref, slot)
        wait_gather(slot)
        if not last:
            start_gather(src_next_ref, 1 - slot)
        if not first:
            start_scatter(dst_prev_ref, 1 - slot)
        rows = xs_ref[slot]
        x = rows[:, 0:D_MODEL].astype(BF16)
        meta = rows[:, D_MODEL:]
        lane = lax.broadcasted_iota(jnp.int32, meta.shape, 1)
        first_expert = tile_group_ref[i] * EXPERTS_PER_GROUP
        acc = jnp.zeros((tm, D_MODEL), F32)
        for e in range(EXPERTS_PER_GROUP):
            w_e = jnp.sum(jnp.where(lane == first_expert + e, meta, 0.0), axis=1, keepdims=True)
            hg = _dot(x, wg_ref[e])
            hu = _dot(x, wu_ref[e])
            act = (_silu(hg) * hu * w_e).astype(BF16)
            acc = acc + _dot(act, wd_ref[e])

        @pl.when(i >= 2)
        def _():
            wait_scatter(slot)

        ys_ref[slot] = acc
        if last:
            start_scatter(dst_ref, slot)
            wait_scatter(1 - slot)
            wait_scatter(slot)

    @pl.when(i == 0)
    def _():
        step(True, False)

    @pl.when((i > 0) & (i < last_step))
    def _():
        step(False, False)

    @pl.when(i == last_step)
    def _():
        step(False, True)


def _moe_plan(tx, counts, tm):
    t = tx.shape[0]
    group = tx[:, D_MODEL + META_GROUP_LANE].astype(jnp.int32)
    count = counts[0, :MOE_GROUPS].astype(jnp.int32)
    padded = (count + tm - 1) // tm * tm
    ends = jnp.cumsum(padded)
    starts = ends - padded
    first_of_group = jnp.cumsum(count) - count
    tiles = t // tm + MOE_GROUPS
    tile_start = jnp.arange(tiles, dtype=jnp.int32) * tm
    tile_group = jnp.sum((tile_start[:, None] >= ends[None, :]).astype(jnp.int32), axis=1)
    tile_group = jnp.minimum(tile_group, MOE_GROUPS - 1)
    order = jnp.argsort(group, stable=True).astype(jnp.int32)
    row_group = jnp.repeat(tile_group, tm)
    k = jnp.arange(tiles * tm, dtype=jnp.int32) - starts[row_group]
    is_real = k < count[row_group]
    src = jnp.where(is_real, order[jnp.clip(first_of_group[row_group] + k, 0, t - 1)], 0)
    spare = t - 1 + jnp.cumsum((~is_real).astype(jnp.int32))
    dst = jnp.where(is_real, src, spare)
    return tile_group, src.reshape(tiles, 1, tm), dst.reshape(tiles, 1, tm)


def _experts(tx, tile_group, src, dst, w_gate, w_up, w_down, layer):
    tiles, _, tm = src.shape
    assert tiles >= 3
    out_rows = tiles * tm
    idx_spec = lambda shift: pl.BlockSpec(
        (1, 1, tm), lambda i, tg: (jnp.clip(i + shift, 0, tiles - 1), 0, 0),
        memory_space=pltpu.SMEM)
    w_spec = lambda a, b: pl.BlockSpec((None, None, EXPERTS_PER_GROUP, a, b),
                                       lambda i, tg: (layer, tg[i], 0, 0, 0))
    grid_spec = pltpu.PrefetchScalarGridSpec(
        num_scalar_prefetch=1,
        grid=(tiles,),
        in_specs=[idx_spec(0), idx_spec(1), idx_spec(0), idx_spec(-1),
                  pl.BlockSpec(memory_space=pl.ANY),
                  w_spec(D_MODEL, D_EXPERT), w_spec(D_MODEL, D_EXPERT), w_spec(D_EXPERT, D_MODEL)],
        out_specs=pl.BlockSpec(memory_space=pl.ANY),
        scratch_shapes=[pltpu.VMEM((2, tm, D_MODEL + LANES), F32),
                        pltpu.VMEM((2, tm, D_MODEL), F32),
                        pltpu.SemaphoreType.DMA((2,)), pltpu.SemaphoreType.DMA((2,))],
    )
    return pl.pallas_call(
        functools.partial(_experts_kernel, tm=tm),
        grid_spec=grid_spec,
        out_shape=jax.ShapeDtypeStruct((out_rows, D_MODEL), F32),
        compiler_params=_cparams(("arbitrary",)),
        name="experts",
    )(tile_group, src, src, dst, dst, tx, w_gate, w_up, w_down)


def _ple_kernel(x_ref, moe_ref, pp_ref, ps_ref, g_ref, wg_ref, wp_ref, gf_ref, *out_refs,
                prompt_tiles, final):
    i = pl.program_id(0)
    x = x_ref[...] + moe_ref[...]
    h = _rmsnorm(x, g_ref[...]).astype(BF16)
    gate = _sigmoid(_dot(h, wg_ref[...]))
    p = jnp.where(i < prompt_tiles, pp_ref[...], ps_ref[...])
    out = x + gate * _dot(p.astype(BF16), wp_ref[...])
    results = [out, _rmsnorm(out, gf_ref[...])] if final else [out]

    @pl.when(i < prompt_tiles)
    def _():
        for n, val in enumerate(results):
            out_refs[2 * n][...] = val

    @pl.when(i == prompt_tiles)
    def _():
        for n, val in enumerate(results):
            out_refs[2 * n + 1][...] = val


def _ple(x, moe, p_prompt, p_sample, g, w_gate, w_proj, g_final, layer, final):
    t = x.shape[0]
    tm = TOKEN_TILE
    tp, ts = p_prompt.shape[1], p_sample.shape[1]
    prompt_tiles = tp // tm
    assert ts == tm and prompt_tiles + 1 == t // tm
    row = lambda width: pl.BlockSpec((tm, width), lambda i: (i, 0))
    par = lambda shape: pl.BlockSpec(shape, lambda i: (0,) * len(shape))
    p_idx = lambda i: jnp.minimum(i, prompt_tiles - 1)
    n_res = 2 if final else 1
    out_specs = [pl.BlockSpec((tm, D_MODEL), lambda i: (p_idx(i), 0)),
                 pl.BlockSpec((tm, D_MODEL), lambda i: (0, 0))] * n_res
    out_shape = [jax.ShapeDtypeStruct((tp, D_MODEL), F32),
                 jax.ShapeDtypeStruct((ts, D_MODEL), F32)] * n_res
    return pl.pallas_call(
        functools.partial(_ple_kernel, prompt_tiles=prompt_tiles, final=final),
        grid=(t // tm,),
        in_specs=[row(D_MODEL), row(D_MODEL),
                  pl.BlockSpec((None, tm, PLE_DIM), lambda i: (layer, p_idx(i), 0)),
                  pl.BlockSpec((None, tm, PLE_DIM), lambda i: (layer, 0, 0)),
                  par((1, D_MODEL)), par((D_MODEL, D_MODEL)), par((PLE_DIM, D_MODEL)),
                  par((1, D_MODEL))],
        out_specs=out_specs,
        out_shape=out_shape,
        compiler_params=_cparams(("arbitrary",)),
        name="ple",
    )(x, moe, p_prompt, p_sample, g, w_gate, w_proj, g_final)


def _pad_rows(a, rows):
    return jnp.pad(a, ((0, 0), (rows - a.shape[1], 0), (0, 0)))


def _layer(i, x_p, x_s, p_prompt, p_sample, cache_k, cache_v, page_table, state_ssm,
           state_ssm_conv, state_cf_conv, prm, dims, final):
    bp, sp_len, bs, ss_len = dims
    tp = bp * sp_len
    ts = bs * ss_len
    row = lambda v: v.reshape(1, -1)

    w_t = jnp.transpose(prm['w_in'][i])
    dt_rows = jnp.pad(w_t[OFF_DT:OFF_GLU], ((0, LANES - SSM_HEADS), (0, 0)))
    w_main_t = jnp.concatenate([w_t[:OFF_DT], w_t[OFF_GLU:OFF_GATE], dt_rows],
                               axis=0).astype(BF16)
    w_gates_t = w_t[OFF_GATE:].astype(BF16)
    g_mix = row(prm['norm_mix'][i])

    qb_p, kt_p, vt_p, ktb_p, vtb_p, z_p, xbc_p, glu_p, dt_p = _in_proj_prompt(
        x_p, g_mix, w_main_t, bp, sp_len)
    qb_s, k_s, v_s, kb_s, vb_s, z_s, xbc_s, glu_s, dt_s = _in_proj_rows(
        x_s, g_mix, w_main_t, 0, ts)

    a_prompt = _attn_prompt(qb_p, ktb_p, vtb_p, prm['sb_bias'][i], bp, sp_len)
    seq3 = lambda a: a.reshape(bs, ss_len, MIX_WIDTH)
    a_sample = _attn_decode(seq3(qb_s), seq3(kb_s), seq3(vb_s), cache_k, cache_v, page_table,
                            prm['sb_bias'][i], i).reshape(ts, MIX_WIDTH)

    pad_lanes = lambda vec, fill: jnp.pad(vec, (0, LANES - vec.shape[0]),
                                          constant_values=fill).reshape(1, LANES)
    ssd_prm = (prm['ssm_conv_w'][i], row(prm['ssm_conv_b'][i]), pad_lanes(prm['dt_bias'][i], 0.0),
               pad_lanes(-jnp.exp(prm['a_log'][i]), 0.0),
               row(jnp.repeat(prm['d_skip'][i], SSM_HEAD_DIM)), row(prm['ssm_norm'][i]))
    zeros_conv = jnp.zeros((bp, SUBLANES, SSM_CONV_CH), F32)
    zeros_h = jnp.zeros((bp, SSM_HEADS, SSM_HEAD_DIM, SSM_STATE), F32)
    b_prompt, sconv_p, h_p = _ssd(xbc_p, dt_p, z_p, zeros_conv, zeros_h, *ssd_prm,
                                  batch=bp, seq=sp_len, valid=sp_len)
    padseq = lambda a: jnp.pad(a.reshape(bs, ss_len, -1),
                               ((0, 0), (0, SSM_CHUNK - ss_len), (0, 0))).reshape(bs * SSM_CHUNK, -1)
    b_s, sconv_s, h_s = _ssd(padseq(xbc_s), padseq(dt_s), padseq(z_s),
                             _pad_rows(state_ssm_conv[i], SUBLANES), state_ssm[i], *ssd_prm,
                             batch=bs, seq=SSM_CHUNK, valid=ss_len)
    b_sample = b_s.reshape(bs, SSM_CHUNK, MIX_WIDTH)[:, :ss_len].reshape(ts, MIX_WIDTH)

    cf_prm = (prm['cf_conv_w'][i], row(prm['cf_conv_b'][i]), row(prm['cf_ln_g'][i]),
              row(prm['cf_ln_b'][i]))
    c_prompt, cf_p = _conformer(glu_p, jnp.zeros((bp, CF_HALO, MIX_WIDTH), F32), *cf_prm,
                                batch=bp, seq=sp_len)
    c_sample, cf_s = _conformer(glu_s, _pad_rows(state_cf_conv[i], CF_HALO), *cf_prm,
                                batch=bs, seq=ss_len)

    x = _merge(x_p, x_s, (a_prompt, b_prompt, c_prompt), (a_sample, b_sample, c_sample), g_mix,
               w_gates_t, prm['w_branch'][i].astype(BF16), prm['w_out'][i].astype(BF16))

    w_router = jnp.concatenate(
        [prm['w_router_expert'][i].reshape(D_MODEL, N_EXPERTS), prm['w_router_group'][i],
         jnp.zeros((D_MODEL, LANES - N_EXPERTS - MOE_GROUPS), F32)], axis=1)
    b_router = jnp.concatenate(
        [prm['b_router_expert'][i].reshape(N_EXPERTS), prm['b_router_group'][i],
         jnp.zeros((LANES - N_EXPERTS - MOE_GROUPS,), F32)]).reshape(1, LANES)
    tx, counts = _router(x, row(prm['norm_ffn'][i]), w_router, b_router)
    tile_group, src, dst = _moe_plan(tx, counts, MOE_ROW_TILE)
    moe = _experts(tx, tile_group, src, dst, prm['w_expert_gate_g'], prm['w_expert_up_g'],
                   prm['w_expert_down_g'], layer=i)

    outs = _ple(x, moe, p_prompt, p_sample, row(prm['norm_ple'][i]),
                prm['w_ple_gate'][i].astype(BF16), prm['w_ple_proj'][i].astype(BF16),
                row(prm['norm_final']), layer=i, final=final)

    conv_tail = SSM_CONV - 1
    cf_tail = CF_CONV - 1
    states = dict(
        k_p=jnp.transpose(kt_p.reshape(bp, SB_HEADS, SB_HEAD_DIM, sp_len), (0, 3, 1, 2)),
        v_p=jnp.transpose(vt_p.reshape(bp, SB_HEADS, SB_HEAD_DIM, sp_len), (0, 3, 1, 2)),
        h_p=h_p, sconv_p=sconv_p[:, SUBLANES - conv_tail:], cf_p=cf_p[:, CF_HALO - cf_tail:],
        k_s=k_s.reshape(bs, ss_len, SB_HEADS, SB_HEAD_DIM),
        v_s=v_s.reshape(bs, ss_len, SB_HEADS, SB_HEAD_DIM),
        h_s=h_s, sconv_s=sconv_s[:, SUBLANES - conv_tail:], cf_s=cf_s[:, CF_HALO - cf_tail:])
    return outs, states


def kernel(x_prompt, x_sample, p_prompt, p_sample, cache_k, cache_v, page_table, state_ssm, state_ssm_conv, state_cf_conv, norm_mix, w_in, sb_bias, ssm_conv_w, ssm_conv_b, dt_bias, a_log, d_skip, ssm_norm, cf_conv_w, cf_conv_b, cf_ln_g, cf_ln_b, w_branch, w_out, norm_ffn, w_router_group, b_router_group, w_router_expert, b_router_expert, w_expert_gate, w_expert_up, w_expert_down, norm_ple, w_ple_gate, w_ple_proj, norm_final):
    prm = dict(norm_mix=norm_mix, w_in=w_in, sb_bias=sb_bias, ssm_conv_w=ssm_conv_w,
               ssm_conv_b=ssm_conv_b, dt_bias=dt_bias, a_log=a_log, d_skip=d_skip,
               ssm_norm=ssm_norm, cf_conv_w=cf_conv_w, cf_conv_b=cf_conv_b, cf_ln_g=cf_ln_g,
               cf_ln_b=cf_ln_b, w_branch=w_branch, w_out=w_out, norm_ffn=norm_ffn,
               w_router_group=w_router_group, b_router_group=b_router_group,
               w_router_expert=w_router_expert, b_router_expert=b_router_expert,
               w_expert_gate=w_expert_gate, w_expert_up=w_expert_up,
               w_expert_down=w_expert_down, norm_ple=norm_ple, w_ple_gate=w_ple_gate,
               w_ple_proj=w_ple_proj, norm_final=norm_final)
    depth = w_in.shape[0]
    for name in ('w_expert_gate', 'w_expert_up', 'w_expert_down'):
        w = prm[name]
        prm[name + '_g'] = w.astype(BF16).reshape(
            (depth, MOE_GROUPS, EXPERTS_PER_GROUP) + w.shape[2:])
    bp, sp_len, _ = x_prompt.shape
    bs, ss_len, _ = x_sample.shape
    tp = bp * sp_len
    ts = bs * ss_len
    dims = (bp, sp_len, bs, ss_len)
    n_pool = cache_k.shape[1]
    ck = jnp.transpose(cache_k, (0, 1, 3, 4, 2)).reshape(depth, n_pool, MIX_WIDTH, PAGE_SIZE)
    cv = jnp.transpose(cache_v, (0, 1, 3, 4, 2)).reshape(depth, n_pool, MIX_WIDTH, PAGE_SIZE)

    x_p = x_prompt.reshape(tp, D_MODEL)
    x_s = x_sample.reshape(ts, D_MODEL)
    pp = p_prompt.reshape(depth, tp, PLE_DIM)
    ps = p_sample.reshape(depth, ts, PLE_DIM)
    per_layer = []
    for i in range(depth):
        outs, st = _layer(i, x_p, x_s, pp, ps, ck, cv, page_table, state_ssm, state_ssm_conv,
                          state_cf_conv, prm, dims, final=(i == depth - 1))
        x_p, x_s = outs[0], outs[1]
        per_layer.append(st)
    y_p, y_s = outs[2], outs[3]
    stack = lambda name: jnp.stack([st[name] for st in per_layer])
    return (y_p.reshape(bp, sp_len, D_MODEL), y_s.reshape(bs, ss_len, D_MODEL),
            stack('k_p'), stack('v_p'), stack('h_p'), stack('sconv_p'), stack('cf_p'),
            stack('k_s'), stack('v_s'), stack('h_s'), stack('sconv_s'), stack('cf_s'))
```
